```python
import math
import jax
import jax.numpy as jnp
from jax import lax
from jax.lax import linalg as lax_linalg
import numpy as np

D_MODEL = 1024
BATCH = 32
SEQ = 2048
DEPTH = 4
DEC_BATCH = 16
DEC_SEQ = 32
PAST_LEN = 1024

CHUNK = 64
Q_BLOCK = 128
MLA_HEADS = 4
MLA_NOPE = 64
MLA_ROPE = 32
MLA_V = 64
MLA_Q_LORA = 384
MLA_KV_LORA = 256
ROPE_THETA = 10000.0
GDN_HEADS = 4
GDN_DK = 128
GDN_DV = 128
CONV_W = 4
FOX_HEADS = 4
FOX_HD = 64
FOX_BIAS_INIT = 3.0
N_MEM = 256
MEM_HEADS = 4
MEM_HD = 128
D_FF = 4 * D_MODEL
ALPHA = (2 * DEPTH) ** 0.25
BETA = (8 * DEPTH) ** -0.25
LN_EPS = 1e-5
RMS_EPS = 1e-6

GDN_QK = GDN_HEADS * GDN_DK
GDN_V = GDN_HEADS * GDN_DV
GDN_CONV_DIM = 2 * GDN_QK + GDN_V
FOX_W = FOX_HEADS * FOX_HD
MLA_W = MLA_HEADS * MLA_V
MIX_WIDTH = MLA_W + GDN_V + FOX_W
MEM_W = MEM_HEADS * MEM_HD
IN_SPLITS = (MLA_Q_LORA, MLA_KV_LORA + MLA_ROPE, GDN_QK, GDN_QK, GDN_V, GDN_V, GDN_HEADS, GDN_HEADS,
             FOX_W, FOX_W, FOX_W, FOX_HEADS)
IN_DIM = sum(IN_SPLITS)

kernel_name = 'hybrid_streaming_encoder_step'


def split_cols(x, sizes):
    offs = np.cumsum(sizes)[:-1].tolist()
    return jnp.split(x, offs, axis=-1)


def layer_norm(x, g, b):
    xf = x.astype(jnp.float32)
    mu = jnp.mean(xf, -1, keepdims=True)
    xc = xf - mu
    var = jnp.mean(xc * xc, -1, keepdims=True)
    return (xc * lax.rsqrt(var + LN_EPS) * g.astype(jnp.float32) + b.astype(jnp.float32)).astype(x.dtype)


def rms_norm(x, g):
    xf = x.astype(jnp.float32)
    y = xf * lax.rsqrt(jnp.mean(xf * xf, -1, keepdims=True) + RMS_EPS)
    return (y * g.astype(jnp.float32)).astype(x.dtype)


def l2_norm(x):
    xf = x.astype(jnp.float32)
    return xf * lax.rsqrt(jnp.sum(xf * xf, -1, keepdims=True) + 1e-6)


def rope(x, pos):
    half = MLA_ROPE // 2
    inv = ROPE_THETA ** (-jnp.arange(half, dtype=jnp.float32) * (2.0 / MLA_ROPE))
    ang = pos.astype(jnp.float32)[:, None] * inv[None, :]
    cos = jnp.cos(ang)[None, :, None, :]
    sin = jnp.sin(ang)[None, :, None, :]
    xf = x.astype(jnp.float32)
    x1, x2 = xf[..., :half], xf[..., half:]
    return jnp.concatenate([x1 * cos - x2 * sin, x2 * cos + x1 * sin], -1).astype(x.dtype)


def chunk_causal(t, s):
    return (s // CHUNK) <= (t // CHUNK)


def frame_causal(t, s):
    return s <= t


def attention(q, k, v, mask_fn, fcum=None):
    Tq = q.shape[1]
    L = k.shape[1]
    q_start = L - Tq
    scale = q.shape[-1] ** -0.5
    outs = []
    for lo in range(0, Tq, Q_BLOCK):
        hi = min(Tq, lo + Q_BLOCK)
        kend = q_start + hi
        qp = q_start + jnp.arange(lo, hi)
        kp = jnp.arange(kend)
        s = jnp.einsum('bqhd,bkhd->bhqk', q[:, lo:hi], k[:, :kend]).astype(jnp.float32) * scale
        if fcum is not None:
            fq = jnp.swapaxes(fcum[:, q_start + lo:q_start + hi], 1, 2)
            fk = jnp.swapaxes(fcum[:, :kend], 1, 2)
            s = s + (fq[..., :, None] - fk[..., None, :])
        s = jnp.where(mask_fn(qp[:, None], kp[None, :]), s, -jnp.inf)
        p = jax.nn.softmax(s, axis=-1).astype(v.dtype)
        outs.append(jnp.einsum('bhqk,bkhd->bqhd', p, v[:, :kend]))
    return jnp.concatenate(outs, axis=1)


def gated_delta_chunked(q, k, v, g, beta, s0, chunk):
    B, T, H, DK = q.shape
    DV = v.shape[-1]
    n = T // chunk
    f32 = jnp.float32

    def to_chunks(a):
        a = a.astype(f32).reshape((B, n, chunk, H) + a.shape[3:])
        return jnp.moveaxis(a, (1, 3), (0, 2))

    qc = to_chunks(q) * (DK ** -0.5)
    kc, vc = to_chunks(k), to_chunks(v)
    gc, bc = to_chunks(g), to_chunks(beta)
    G = jnp.cumsum(gc, axis=-1)
    idx = jnp.arange(chunk)
    incl = idx[:, None] >= idx[None, :]
    strict = idx[:, None] > idx[None, :]
    decay = jnp.exp(jnp.where(incl, G[..., :, None] - G[..., None, :], -jnp.inf))
    kk = jnp.einsum('nbhik,nbhjk->nbhij', kc, kc)
    a_mat = jnp.where(strict, bc[..., :, None] * kk * decay, 0.0)
    m_mat = jnp.eye(chunk, dtype=f32) + a_mat
    rhs = jnp.concatenate([vc * bc[..., None], kc * (bc * jnp.exp(G))[..., None]], -1)
    sol = lax_linalg.triangular_solve(m_mat, rhs, left_side=True, lower=True, unit_diagonal=True)
    u, w = sol[..., :DV], sol[..., DV:]
    qk = jnp.einsum('nbhik,nbhjk->nbhij', qc, kc) * decay
    q_dec = qc * jnp.exp(G)[..., None]
    k_dec = kc * jnp.exp(G[..., -1:] - G)[..., None]
    g_last = jnp.exp(G[..., -1])

    def step(S, xs):
        u_i, w_i, qk_i, qd_i, kd_i, gl_i = xs
        v_new = u_i - jnp.einsum('bhck,bhkv->bhcv', w_i, S)
        o = jnp.einsum('bhck,bhkv->bhcv', qd_i, S) + jnp.einsum('bhij,bhjv->bhiv', qk_i, v_new)
        S = S * gl_i[..., None, None] + jnp.einsum('bhck,bhcv->bhkv', kd_i, v_new)
        return S, o

    S, o = lax.scan(step, s0.astype(f32), (u, w, qk, q_dec, k_dec, g_last))
    o = jnp.moveaxis(o, (0, 2), (1, 3)).reshape(B, T, H, DV)
    return o, S


def gated_deltanet(gq, gk, gv, gz, ga, gb, conv_past, s_past, conv_w, a_log, dt_bias, norm_g):
    B, T, _ = gq.shape
    qkv = jnp.concatenate([gq, gk, gv], -1)
    qkv_full = jnp.concatenate([conv_past.astype(qkv.dtype), qkv], 1)
    y = lax.conv_general_dilated(qkv_full, conv_w[:, None, :].astype(qkv.dtype), (1,), 'VALID',
                                 dimension_numbers=('NWC', 'WIO', 'NWC'),
                                 feature_group_count=GDN_CONV_DIM)
    y = jax.nn.silu(y)
    cq, ck, cv = split_cols(y, (GDN_QK, GDN_QK, GDN_V))
    q = l2_norm(cq.reshape(B, T, GDN_HEADS, GDN_DK))
    k = l2_norm(ck.reshape(B, T, GDN_HEADS, GDN_DK))
    v = cv.reshape(B, T, GDN_HEADS, GDN_DV)
    g = -jnp.exp(a_log.astype(jnp.float32)) * jax.nn.softplus(ga.astype(jnp.float32) + dt_bias.astype(jnp.float32))
    beta = jax.nn.sigmoid(gb.astype(jnp.float32))
    o, s_new = gated_delta_chunked(q, k, v, g, beta, s_past, min(CHUNK, T))
    z = gz.reshape(B, T, GDN_HEADS, GDN_DV).astype(jnp.float32)
    o = rms_norm(o, norm_g) * jax.nn.silu(z)
    return o.reshape(B, T, GDN_V).astype(gq.dtype), s_new.astype(s_past.dtype), qkv_full[:, -(CONV_W - 1):]


def memory_attention(x, mem_k, mem_v, w_xq, w_xo):
    B, T, _ = x.shape
    q = (x @ w_xq).reshape(B, T, MEM_HEADS, MEM_HD)
    s = jnp.einsum('bthd,bmhd->bhtm', q, mem_k).astype(jnp.float32) * (MEM_HD ** -0.5)
    p = jax.nn.softmax(s, axis=-1).astype(mem_v.dtype)
    o = jnp.einsum('bhtm,bmhd->bthd', p, mem_v).reshape(B, T, MEM_W)
    return o @ w_xo


def encoder_layer(x, past, mem_k, mem_v, lp):
    (w_in, qa_g, kva_g, w_uq, w_ukv, conv_w, a_log, dt_bias, gdn_g, fox_bf,
     w_out, w_xq, w_xo, w_ff1, w_ff2, ln_g, ln_b) = lp
    ckv_past, kpe_past, fk_past, fv_past, logf_past, s_past, conv_past = past
    B, T, _ = x.shape
    dt = x.dtype
    q_start = ckv_past.shape[1]
    pos = q_start + jnp.arange(T)
    (cq, kva, gq, gk, gv, gz, ga, gb, fq, fk, fv, ff) = split_cols(x @ w_in, IN_SPLITS)

    c_q = rms_norm(cq, qa_g)
    q_a = (c_q @ w_uq).reshape(B, T, MLA_HEADS, MLA_NOPE + MLA_ROPE)
    q_a = jnp.concatenate([q_a[..., :MLA_NOPE], rope(q_a[..., MLA_NOPE:], pos)], -1)
    c_kv = rms_norm(kva[..., :MLA_KV_LORA], kva_g)
    k_pe = rope(kva[:, :, None, MLA_KV_LORA:], pos)[:, :, 0]
    ckv_all = jnp.concatenate([ckv_past, c_kv], 1)
    kpe_all = jnp.concatenate([kpe_past, k_pe], 1)
    L = ckv_all.shape[1]
    kv_a = (ckv_all @ w_ukv).reshape(B, L, MLA_HEADS, MLA_NOPE + MLA_V)
    k_a = jnp.concatenate([kv_a[..., :MLA_NOPE],
                           jnp.broadcast_to(kpe_all[:, :, None, :], (B, L, MLA_HEADS, MLA_ROPE))], -1)
    o_a = attention(q_a, k_a, kv_a[..., MLA_NOPE:], chunk_causal)

    o_b, s_new, conv_new = gated_deltanet(gq, gk, gv, gz, ga, gb, conv_past, s_past,
                                          conv_w, a_log, dt_bias, gdn_g)

    fq = fq.reshape(B, T, FOX_HEADS, FOX_HD)
    fk = fk.reshape(B, T, FOX_HEADS, FOX_HD)
    fv = fv.reshape(B, T, FOX_HEADS, FOX_HD)
    logf = jax.nn.log_sigmoid(ff.astype(jnp.float32) + fox_bf.astype(jnp.float32))
    fk_all = jnp.concatenate([fk_past, fk], 1)
    fv_all = jnp.concatenate([fv_past, fv], 1)
    fcum = jnp.cumsum(jnp.concatenate([logf_past.astype(jnp.float32), logf], 1), axis=1)
    o_c = attention(fq, fk_all, fv_all, frame_causal, fcum)

    mix = jnp.concatenate([o_a.reshape(B, T, MLA_W), o_b, o_c.reshape(B, T, FOX_W)], -1) @ w_out
    x = layer_norm(ALPHA * x + mix, ln_g[0], ln_b[0])
    x = layer_norm(ALPHA * x + memory_attention(x, mem_k, mem_v, w_xq, w_xo), ln_g[1], ln_b[1])
    ffn = jnp.square(jax.nn.relu(x @ w_ff1)) @ w_ff2
    x = layer_norm(ALPHA * x + ffn, ln_g[2], ln_b[2])
    return x, (c_kv, k_pe, fk, fv, logf.astype(dt), s_new, conv_new)


def _stack(entries, i):
    return jnp.stack([e[i] for e in entries])


def setup_inputs(seed: int = 0) -> dict:
    key = jax.random.key(seed)
    ks = jax.random.split(key, 34)
    f32 = jnp.float32

    def nrm(i, shape, scale=1.0):
        return jax.random.normal(ks[i], shape, f32) * scale

    col_scale = jnp.concatenate([
        jnp.ones((MLA_Q_LORA + MLA_KV_LORA + MLA_ROPE + 2 * GDN_QK,), f32),
        jnp.full((GDN_V,), BETA, f32),
        jnp.ones((GDN_V + 2 * GDN_HEADS + 2 * FOX_W,), f32),
        jnp.full((FOX_W,), BETA, f32),
        jnp.ones((FOX_HEADS,), f32)])
    ukv_scale = jnp.tile(jnp.concatenate([jnp.ones((MLA_NOPE,), f32), jnp.full((MLA_V,), BETA, f32)]), MLA_HEADS)
    dt = jnp.exp(jax.random.uniform(ks[21], (DEPTH, GDN_HEADS), f32, minval=math.log(1e-3), maxval=math.log(1e-1)))
    return {
        'x_prompt': nrm(0, (BATCH, SEQ, D_MODEL)),
        'x_sample': nrm(1, (DEC_BATCH, DEC_SEQ, D_MODEL)),
        'cache_mla_ckv': nrm(2, (DEPTH, DEC_BATCH, PAST_LEN, MLA_KV_LORA)),
        'cache_mla_kpe': nrm(3, (DEPTH, DEC_BATCH, PAST_LEN, MLA_ROPE)),
        'cache_fox_k': nrm(4, (DEPTH, DEC_BATCH, PAST_LEN, FOX_HEADS, FOX_HD)),
        'cache_fox_v': nrm(5, (DEPTH, DEC_BATCH, PAST_LEN, FOX_HEADS, FOX_HD), 0.5),
        'cache_fox_logf': jax.nn.log_sigmoid(FOX_BIAS_INIT + nrm(6, (DEPTH, DEC_BATCH, PAST_LEN, FOX_HEADS))),
        'state_gdn': nrm(7, (DEPTH, DEC_BATCH, GDN_HEADS, GDN_DK, GDN_DV), 0.1),
        'state_gdn_conv': nrm(8, (DEPTH, DEC_BATCH, CONV_W - 1, GDN_CONV_DIM)),
        'cache_mem_k': nrm(9, (DEPTH, DEC_BATCH, N_MEM, MEM_HEADS, MEM_HD)),
        'cache_mem_v': nrm(10, (DEPTH, DEC_BATCH, N_MEM, MEM_HEADS, MEM_HD), 0.5),
        'mem_prompt': nrm(11, (BATCH, N_MEM, D_MODEL)),
        'ln_in_g': 1.0 + nrm(12, (D_MODEL,), 0.02),
        'ln_in_b': nrm(13, (D_MODEL,), 0.02),
        'w_in': nrm(14, (DEPTH, D_MODEL, IN_DIM), D_MODEL ** -0.5) * col_scale,
        'qa_g': 1.0 + nrm(15, (DEPTH, MLA_Q_LORA), 0.02),
        'kva_g': 1.0 + nrm(16, (DEPTH, MLA_KV_LORA), 0.02),
        'w_uq': nrm(17, (DEPTH, MLA_Q_LORA, MLA_HEADS * (MLA_NOPE + MLA_ROPE)), MLA_Q_LORA ** -0.5),
        'w_ukv': nrm(18, (DEPTH, MLA_KV_LORA, MLA_HEADS * (MLA_NOPE + MLA_V)), MLA_KV_LORA ** -0.5) * ukv_scale,
        'gdn_conv_w': nrm(19, (DEPTH, CONV_W, GDN_CONV_DIM), CONV_W ** -0.5),
        'gdn_a_log': jnp.log(jax.random.uniform(ks[20], (DEPTH, GDN_HEADS), f32, minval=1.0, maxval=16.0)),
        'gdn_dt_bias': dt + jnp.log(-jnp.expm1(-dt)),
        'gdn_norm_g': 1.0 + nrm(22, (DEPTH, GDN_DV), 0.02),
        'fox_bf': FOX_BIAS_INIT + nrm(23, (DEPTH, FOX_HEADS), 0.1),
        'w_out': nrm(24, (DEPTH, MIX_WIDTH, D_MODEL), MIX_WIDTH ** -0.5 * BETA),
        'w_xq': nrm(25, (DEPTH, D_MODEL, MEM_W), D_MODEL ** -0.5),
        'w_mk': nrm(26, (DEPTH, D_MODEL, MEM_W), D_MODEL ** -0.5),
        'w_mv': nrm(27, (DEPTH, D_MODEL, MEM_W), D_MODEL ** -0.5 * BETA),
        'w_xo': nrm(28, (DEPTH, MEM_W, D_MODEL), MEM_W ** -0.5 * BETA),
        'w_ff1': nrm(29, (DEPTH, D_MODEL, D_FF), D_MODEL ** -0.5),
        'w_ff2': nrm(30, (DEPTH, D_FF, D_MODEL), D_FF ** -0.5 * BETA),
        'ln_g': 1.0 + nrm(31, (DEPTH, 3, D_MODEL), 0.02),
        'ln_b': nrm(32, (DEPTH, 3, D_MODEL), 0.02),
    }


def reference(x_prompt, x_sample, cache_mla_ckv, cache_mla_kpe, cache_fox_k, cache_fox_v, cache_fox_logf,
              state_gdn, state_gdn_conv, cache_mem_k, cache_mem_v, mem_prompt, ln_in_g, ln_in_b,
              w_in, qa_g, kva_g, w_uq, w_ukv, gdn_conv_w, gdn_a_log, gdn_dt_bias, gdn_norm_g, fox_bf,
              w_out, w_xq, w_mk, w_mv, w_xo, w_ff1, w_ff2, ln_g, ln_b):
    xp = layer_norm(x_prompt, ln_in_g, ln_in_b)
    xs = layer_norm(x_sample, ln_in_g, ln_in_b)
    dt = xp.dtype
    bp = x_prompt.shape[0]
    empty_past = (jnp.zeros((bp, 0, MLA_KV_LORA), dt), jnp.zeros((bp, 0, MLA_ROPE), dt),
                  jnp.zeros((bp, 0, FOX_HEADS, FOX_HD), dt), jnp.zeros((bp, 0, FOX_HEADS, FOX_HD), dt),
                  jnp.zeros((bp, 0, FOX_HEADS), dt), jnp.zeros((bp, GDN_HEADS, GDN_DK, GDN_DV), dt),
                  jnp.zeros((bp, CONV_W - 1, GDN_CONV_DIM), dt))
    p_new, p_mem, s_new = [], [], []
    for l in range(DEPTH):
        lp = (w_in[l], qa_g[l], kva_g[l], w_uq[l], w_ukv[l], gdn_conv_w[l], gdn_a_log[l], gdn_dt_bias[l],
              gdn_norm_g[l], fox_bf[l], w_out[l], w_xq[l], w_xo[l], w_ff1[l], w_ff2[l], ln_g[l], ln_b[l])
        mk = (mem_prompt @ w_mk[l]).reshape(bp, N_MEM, MEM_HEADS, MEM_HD)
        mv = (mem_prompt @ w_mv[l]).reshape(bp, N_MEM, MEM_HEADS, MEM_HD)
        xp, ent_p = encoder_layer(xp, empty_past, mk, mv, lp)
        p_new.append(ent_p)
        p_mem.append((mk, mv))
        past = (cache_mla_ckv[l], cache_mla_kpe[l], cache_fox_k[l], cache_fox_v[l], cache_fox_logf[l],
                state_gdn[l], state_gdn_conv[l])
        xs, ent_s = encoder_layer(xs, past, cache_mem_k[l], cache_mem_v[l], lp)
        s_new.append(ent_s)
    return (xp, xs,
            _stack(p_new, 0), _stack(p_new, 1), _stack(p_new, 2), _stack(p_new, 3), _stack(p_new, 4),
            _stack(p_new, 5), _stack(p_new, 6), _stack(p_mem, 0), _stack(p_mem, 1),
            _stack(s_new, 0), _stack(s_new, 1), _stack(s_new, 2), _stack(s_new, 3), _stack(s_new, 4),
            _stack(s_new, 5), _stack(s_new, 6))
```

```python
import functools
import math

import jax
import jax.numpy as jnp
from jax import lax
from jax.experimental import pallas as pl
from jax.experimental.pallas import tpu as pltpu

F32 = jnp.float32
BF16 = jnp.bfloat16

D_MODEL = 1024
CHUNK = 64
MLA_HEADS = 4
MLA_NOPE = 64
MLA_ROPE = 32
MLA_V = 64
MLA_Q_LORA = 384
MLA_KV_LORA = 256
ROPE_THETA = 10000.0
GDN_HEADS = 4
GDN_DK = 128
GDN_DV = 128
CONV_W = 4
FOX_HEADS = 4
FOX_HD = 64
N_MEM = 256
MEM_HEADS = 4
MEM_HD = 128
D_FF = 4 * D_MODEL
LN_EPS = 1e-5
RMS_EPS = 1e-6
L2_EPS = 1e-6

GDN_QK = GDN_HEADS * GDN_DK
GDN_V = GDN_HEADS * GDN_DV
GDN_CONV_DIM = 2 * GDN_QK + GDN_V
FOX_W = FOX_HEADS * FOX_HD
MLA_W = MLA_HEADS * MLA_V
MEM_W = MEM_HEADS * MEM_HD

LANES = 128
SUBLANES = 8
ROW_TILE = 512
ATTN_BLOCK = 512
SCAN_BLOCK = 256
VMEM_LIMIT = 56 * 1024 * 1024
NEG_BIG = -1e30

FF_LANE = 0
GA_LANE = 4
GB_LANE = 8
ROPE_LO = MLA_NOPE
ROPE_HI = MLA_NOPE + MLA_ROPE


def _params(*sem):
    return pltpu.CompilerParams(dimension_semantics=sem, vmem_limit_bytes=VMEM_LIMIT)


def _resident(shape):
    nd = len(shape)
    return pl.BlockSpec(shape, lambda *_: (0,) * nd)


def _layer_norm(y, g, b):
    mu = jnp.mean(y, axis=-1, keepdims=True)
    yc = y - mu
    var = jnp.mean(yc * yc, axis=-1, keepdims=True)
    return yc * lax.rsqrt(var + LN_EPS) * g + b


def _sigmoid(x):
    return 1.0 / (1.0 + jnp.exp(-x))


def _softplus(x):
    return jnp.maximum(x, 0.0) + jnp.log1p(jnp.exp(-jnp.abs(x)))


def _dot(a, b):
    return jnp.dot(a, b, preferred_element_type=F32)


def _dot_nt(a, b):
    return lax.dot_general(a, b, (((1,), (1,)), ((), ())), preferred_element_type=F32)


def _split3(x):
    hi = x.astype(BF16)
    r1 = x - hi.astype(F32)
    mid = r1.astype(BF16)
    lo = (r1 - mid.astype(F32)).astype(BF16)
    return hi, mid, lo


def _cumsum_rows(tri, x):
    hi, mid, lo = _split3(x)
    return _dot(tri, hi) + _dot(tri, mid) + _dot(tri, lo)


def _tri(n):
    r = lax.broadcasted_iota(jnp.int32, (n, n), 0)
    c = lax.broadcasted_iota(jnp.int32, (n, n), 1)
    return (r >= c).astype(BF16)


def _rope_table_kernel(cos_ref, sin_ref, *, start):
    shape = cos_ref.shape
    lane = lax.broadcasted_iota(jnp.int32, shape, 1)
    row = lax.broadcasted_iota(jnp.int32, shape, 0)
    half = MLA_ROPE // 2
    idx = ((lane - ROPE_LO) & (half - 1)).astype(F32)
    inv = jnp.exp(idx * (-(2.0 / MLA_ROPE) * math.log(ROPE_THETA)))
    ang = (row + start).astype(F32) * inv
    in_rope = (lane >= ROPE_LO) & (lane < ROPE_HI)
    cos_ref[...] = jnp.where(in_rope, jnp.cos(ang), 1.0)
    sin_ref[...] = jnp.where(in_rope, jnp.sin(ang), 0.0)


def _rope_tables(t, start):
    shp = jax.ShapeDtypeStruct((t, LANES), F32)
    return pl.pallas_call(functools.partial(_rope_table_kernel, start=start),
                          out_shape=(shp, shp), name="rope_tables")()


def _ln_kernel(x_ref, g_ref, b_ref, o_ref):
    o_ref[...] = _layer_norm(x_ref[...], g_ref[...], b_ref[...])


def _ln_in(x2d, g, b):
    m, d = x2d.shape
    tm = min(ROW_TILE, m)
    return pl.pallas_call(
        _ln_kernel, grid=(m // tm,),
        in_specs=[pl.BlockSpec((tm, d), lambda i: (i, 0)), _resident((1, d)), _resident((1, d))],
        out_specs=pl.BlockSpec((tm, d), lambda i: (i, 0)),
        out_shape=jax.ShapeDtypeStruct((m, d), F32),
        compiler_params=_params("parallel"), name="ln_in")(x2d, g.reshape(1, d), b.reshape(1, d))


IN_MLA_W = MLA_Q_LORA + MLA_KV_LORA + 2 * LANES
IN_GDN_W = GDN_CONV_DIM + GDN_V
IN_PAD_COLS = (IN_MLA_W, IN_GDN_W, LANES, FOX_W, FOX_W, FOX_W)
IN_PAD_OFFS = tuple(sum(IN_PAD_COLS[:n]) for n in range(len(IN_PAD_COLS) + 1))


def _in_proj_kernel(x_ref, w_ref, *o_refs):
    xb = x_ref[...].astype(BF16)
    for o_ref, lo, hi in zip(o_refs, IN_PAD_OFFS[:-1], IN_PAD_OFFS[1:]):
        o_ref[...] = _dot(xb, w_ref[:, lo:hi])


def _in_proj(x2d, w_pad):
    m, d = x2d.shape
    tm = min(ROW_TILE, m)
    return pl.pallas_call(
        _in_proj_kernel, grid=(m // tm,),
        in_specs=[pl.BlockSpec((tm, d), lambda i: (i, 0)), _resident(w_pad.shape)],
        out_specs=[pl.BlockSpec((tm, w), lambda i: (i, 0)) for w in IN_PAD_COLS],
        out_shape=[jax.ShapeDtypeStruct((m, w), F32) for w in IN_PAD_COLS],
        compiler_params=_params("parallel"), name="in_proj")(x2d, w_pad)


def _pad_w_in(w_in):
    offs = [0]
    for s in (MLA_Q_LORA, MLA_KV_LORA, MLA_ROPE, GDN_QK, GDN_QK, GDN_V, GDN_V, GDN_HEADS, GDN_HEADS,
              FOX_W, FOX_W, FOX_W, FOX_HEADS):
        offs.append(offs[-1] + s)
    cq, ckv, kpe, gq, gk, gv, gz, ga, gb, fq, fk, fv, ff = [w_in[..., a:b] for a, b in zip(offs[:-1], offs[1:])]
    half = MLA_ROPE // 2
    kpe_rot = jnp.concatenate([-kpe[..., half:], kpe[..., :half]], -1)

    def z(n):
        return jnp.zeros(w_in.shape[:-1] + (n,), w_in.dtype)

    def rope_block(w):
        return jnp.concatenate([z(ROPE_LO), w, z(LANES - ROPE_HI)], -1)

    small = jnp.concatenate([ff, ga, gb, z(LANES - 3 * GDN_HEADS)], -1)
    return jnp.concatenate([cq, ckv, rope_block(kpe), rope_block(kpe_rot), gq, gk, gv, gz, small, fq, fk, fv],
                           -1).astype(BF16)


def _rms(x, g):
    return x * lax.rsqrt(jnp.mean(x * x, axis=-1, keepdims=True) + RMS_EPS) * g


def _mla_prep_kernel(p_ref, cos_ref, sin_ref, qg_ref, kg_ref, wq_ref, q_ref, ckv_ref, kpe_ref):
    cos = cos_ref[...]
    sin = sin_ref[...]
    c_q = _rms(p_ref[:, :MLA_Q_LORA], qg_ref[...]).astype(BF16)
    qq = _dot(c_q, wq_ref[...])
    hw = MLA_HEADS * LANES
    for h in range(MLA_HEADS):
        a = qq[:, h * LANES:(h + 1) * LANES]
        b = qq[:, hw + h * LANES:hw + (h + 1) * LANES]
        q_ref[:, h * LANES:(h + 1) * LANES] = (a * cos + b * sin).astype(BF16)
    lo = MLA_Q_LORA
    ckv_ref[...] = _rms(p_ref[:, lo:lo + MLA_KV_LORA], kg_ref[...])
    lo += MLA_KV_LORA
    kpe_ref[...] = p_ref[:, lo:lo + LANES] * cos + p_ref[:, lo + LANES:lo + 2 * LANES] * sin


def _mla_prep(p_mla, cos, sin, qa_g, kva_g, w_uq2, t):
    m = p_mla.shape[0]
    tm = min(ROW_TILE, t)
    nt = t // tm
    row = lambda w: pl.BlockSpec((tm, w), lambda i: (i, 0))
    tab = pl.BlockSpec((tm, LANES), lambda i: (i % nt, 0))
    return pl.pallas_call(
        _mla_prep_kernel, grid=(m // tm,),
        in_specs=[row(IN_MLA_W), tab, tab, _resident((1, MLA_Q_LORA)), _resident((1, MLA_KV_LORA)),
                  _resident(w_uq2.shape)],
        out_specs=[row(MLA_HEADS * LANES), row(MLA_KV_LORA), row(LANES)],
        out_shape=[jax.ShapeDtypeStruct((m, MLA_HEADS * LANES), BF16),
                   jax.ShapeDtypeStruct((m, MLA_KV_LORA), F32),
                   jax.ShapeDtypeStruct((m, LANES), F32)],
        compiler_params=_params("parallel"), name="mla_prep")(
            p_mla, cos, sin, qa_g.reshape(1, -1), kva_g.reshape(1, -1), w_uq2)


def _pad_w_uq(w_uq):
    half = MLA_ROPE // 2
    w = w_uq.reshape(w_uq.shape[:-1] + (MLA_HEADS, MLA_NOPE + MLA_ROPE))
    nope, rp = w[..., :MLA_NOPE], w[..., MLA_NOPE:]
    rot = jnp.concatenate([-rp[..., half:], rp[..., :half]], -1)
    zpad = jnp.zeros(w.shape[:-1] + (LANES - ROPE_HI,), w.dtype)
    plain = jnp.concatenate([nope, rp, zpad], -1)
    rotated = jnp.concatenate([jnp.zeros_like(nope), rot, zpad], -1)
    flat = lambda a: a.reshape(a.shape[:-2] + (MLA_HEADS * LANES,))
    return jnp.concatenate([flat(plain), flat(rotated)], -1).astype(BF16)


def _mla_kv_kernel(ckv_ref, kpe_ref, w_ref, k_ref, v_ref):
    kv = _dot(ckv_ref[...].astype(BF16), w_ref[...])
    kpe = kpe_ref[...]
    kw = MLA_HEADS * LANES
    for h in range(MLA_HEADS):
        k_ref[:, h * LANES:(h + 1) * LANES] = (kv[:, h * LANES:(h + 1) * LANES] + kpe).astype(BF16)
    v_ref[...] = kv[:, kw:].astype(BF16)


def _mla_kv(ckv_all, kpe_all, w_ukv2):
    r = ckv_all.shape[0]
    tm = min(ROW_TILE, r)
    row = lambda w: pl.BlockSpec((tm, w), lambda i: (i, 0))
    return pl.pallas_call(
        _mla_kv_kernel, grid=(r // tm,),
        in_specs=[row(MLA_KV_LORA), row(LANES), _resident(w_ukv2.shape)],
        out_specs=[row(MLA_HEADS * LANES), row(MLA_W)],
        out_shape=[jax.ShapeDtypeStruct((r, MLA_HEADS * LANES), BF16), jax.ShapeDtypeStruct((r, MLA_W), BF16)],
        compiler_params=_params("parallel"), name="mla_kv")(ckv_all, kpe_all, w_ukv2)


def _pad_w_ukv(w_ukv):
    w = w_ukv.reshape(w_ukv.shape[:-1] + (MLA_HEADS, MLA_NOPE + MLA_V))
    nope, v = w[..., :MLA_NOPE], w[..., MLA_NOPE:]
    kpad = jnp.concatenate([nope, jnp.zeros(nope.shape[:-1] + (LANES - MLA_NOPE,), w.dtype)], -1)
    flat = lambda a: a.reshape(a.shape[:-2] + (-1,))
    return jnp.concatenate([flat(kpad), flat(v)], -1).astype(BF16)


def _flash_kernel(*refs, tq, tk, q_start, kv_len, kv_pad, causal, scale, packed_qk, has_f, heads):
    if has_f:
        q_ref, k_ref, v_ref, fq_ref, fk_ref, o_ref, m_sc, l_sc, acc_sc = refs
    else:
        q_ref, k_ref, v_ref, o_ref, m_sc, l_sc, acc_sc = refs
    i = pl.program_id(1)
    j = pl.program_id(2)
    nk = pl.num_programs(2)

    @pl.when(j == 0)
    def _():
        m_sc[...] = jnp.full(m_sc.shape, NEG_BIG, F32)
        l_sc[...] = jnp.zeros(l_sc.shape, F32)
        acc_sc[...] = jnp.zeros(acc_sc.shape, F32)

    qmin = q_start + i * tq
    qmax = qmin + tq - 1
    kmin = j * tk
    kmax = kmin + tk - 1
    if causal == "frame":
        any_vis = kmin <= qmax
        all_vis = kmax <= qmin
    else:
        any_vis = (kmin // CHUNK) <= (qmax // CHUNK)
        all_vis = (kmax // CHUNK) <= (qmin // CHUNK)
    if kv_len < kv_pad:
        all_vis = jnp.logical_and(all_vis, kmax < kv_len)

    fold = math.log2(scale) == round(math.log2(scale))
    half_lane = lax.broadcasted_iota(jnp.int32, (1, LANES), 1) < (LANES // 2)

    def body(masked):
        if masked:
            qpos = qmin + lax.broadcasted_iota(jnp.int32, (tq, tk), 0)
            kpos = kmin + lax.broadcasted_iota(jnp.int32, (tq, tk), 1)
            if causal == "frame":
                vis = kpos <= qpos
            else:
                vis = (kpos // CHUNK) <= (qpos // CHUNK)
            if kv_len < kv_pad:
                vis = jnp.logical_and(vis, kpos < kv_len)
        for pair in range(heads // 2):
            vpair = v_ref[0, :, pair * LANES:(pair + 1) * LANES].astype(BF16)
            pv = None
            alphas = []
            for sub in range(2):
                h = 2 * pair + sub
                mine = half_lane if sub == 0 else jnp.logical_not(half_lane)
                if packed_qk:
                    qh = q_ref[0, :, pair * LANES:(pair + 1) * LANES]
                    qh = jnp.where(mine, qh, 0.0)
                    kh = k_ref[0, :, pair * LANES:(pair + 1) * LANES]
                else:
                    qh = q_ref[0, :, h * LANES:(h + 1) * LANES]
                    kh = k_ref[0, :, h * LANES:(h + 1) * LANES]
                if fold:
                    qh = qh * scale
                s = _dot_nt(qh.astype(BF16), kh.astype(BF16))
                if not fold:
                    s = s * scale
                if has_f:
                    s = s + (fq_ref[0, :, FF_LANE + h:FF_LANE + h + 1] - fk_ref[0, FF_LANE + h:FF_LANE + h + 1, :])
                if masked:
                    s = jnp.where(vis, s, NEG_BIG)
                m_old = m_sc[h]
                m_new = jnp.maximum(m_old, jnp.max(s, axis=-1, keepdims=True))
                alpha = jnp.exp(m_old - m_new)
                e = jnp.exp(s - m_new)
                l_sc[h] = alpha * l_sc[h] + jnp.sum(e, axis=-1, keepdims=True)
                m_sc[h] = m_new
                vh = jnp.where(mine, vpair, jnp.zeros_like(vpair))
                d = _dot(e.astype(BF16), vh)
                pv = d if pv is None else pv + d
                alphas.append(alpha)
            a_sel = jnp.where(half_lane, alphas[0], alphas[1])
            cols = slice(pair * LANES, (pair + 1) * LANES)
            acc_sc[:, cols] = acc_sc[:, cols] * a_sel + pv

    @pl.when(all_vis)
    def _():
        body(False)

    @pl.when(jnp.logical_and(any_vis, jnp.logical_not(all_vis)))
    def _():
        body(True)

    @pl.when(j == nk - 1)
    def _():
        for pair in range(heads // 2):
            inv = jnp.where(half_lane, 1.0 / l_sc[2 * pair], 1.0 / l_sc[2 * pair + 1])
            cols = slice(pair * LANES, (pair + 1) * LANES)
            o_ref[0, :, cols] = (acc_sc[:, cols] * inv).astype(o_ref.dtype)


def _flash(q, k, v, *, causal, scale, packed_qk, kv_len, fq=None, fk=None, tq=None, tk=None):
    b, t, wq = q.shape
    lp = k.shape[1]
    heads = v.shape[2] // (LANES // 2)
    tq = tq or min(ATTN_BLOCK, t)
    tk = tk or (min(ATTN_BLOCK, lp) if t >= ATTN_BLOCK else lp)
    q_start = kv_len - t
    nq, nk = t // tq, lp // tk

    def last_block(i):
        qmax = q_start + i * tq + tq - 1
        last = qmax if causal == "frame" else (qmax // CHUNK) * CHUNK + CHUNK - 1
        return jnp.minimum(last // tk, nk - 1)

    kv_map = lambda bi, i, j: (bi, jnp.minimum(j, last_block(i)), 0)
    in_specs = [pl.BlockSpec((1, tq, wq), lambda bi, i, j: (bi, i, 0)),
                pl.BlockSpec((1, tk, wq), kv_map),
                pl.BlockSpec((1, tk, v.shape[2]), kv_map)]
    args = [q, k, v]
    has_f = fq is not None
    if has_f:
        in_specs += [pl.BlockSpec((1, tq, LANES), lambda bi, i, j: (bi, i + q_start // tq, 0)),
                     pl.BlockSpec((1, SUBLANES, tk), lambda bi, i, j: (bi, 0, jnp.minimum(j, last_block(i))))]
        args += [fq, fk]
        assert q_start % tq == 0
    kern = functools.partial(_flash_kernel, tq=tq, tk=tk, q_start=q_start, kv_len=kv_len, kv_pad=lp,
                             causal=causal, scale=scale, packed_qk=packed_qk, has_f=has_f, heads=heads)
    return pl.pallas_call(
        kern, grid=(b, nq, nk), in_specs=in_specs,
        out_specs=pl.BlockSpec((1, tq, v.shape[2]), lambda bi, i, j: (bi, i, 0)),
        out_shape=jax.ShapeDtypeStruct((b, t, v.shape[2]), BF16),
        scratch_shapes=[pltpu.VMEM((heads, tq, 1), F32), pltpu.VMEM((heads, tq, 1), F32),
                        pltpu.VMEM((tq, v.shape[2]), F32)],
        compiler_params=_params("parallel", "parallel", "arbitrary"),
        name="flash_" + causal)(*args)


def _fox_scan_kernel(*refs, p_len, t_len, l_pad):
    if p_len:
        small_ref, bias_ref, past_ref, logf_ref, fc_ref, fct_ref = refs
    else:
        small_ref, bias_ref, logf_ref, fc_ref, fct_ref = refs
    x = small_ref[0] + bias_ref[...]
    logf = jnp.minimum(x, 0.0) - jnp.log1p(jnp.exp(-jnp.abs(x)))
    logf_ref[0] = logf
    bs = SCAN_BLOCK
    tri = _tri(bs)
    carry = jnp.zeros((1, LANES), F32)
    for blk in range(l_pad // bs):
        lo = blk * bs
        if lo + bs <= p_len:
            xb = past_ref[0, lo:lo + bs, :]
        elif lo >= p_len and lo + bs <= p_len + t_len:
            xb = logf[lo - p_len:lo - p_len + bs]
        else:
            assert lo == p_len and t_len < bs
            xb = jnp.concatenate([logf, jnp.zeros((bs - t_len, LANES), F32)], 0)
        cs = _cumsum_rows(tri, xb) + carry
        carry = cs[bs - 1:bs, :]
        fc_ref[0, lo:lo + bs, :] = cs
        fct_ref[0, :, lo:lo + bs] = cs.T[:SUBLANES, :]


def _fox_scan(small3, bias128, past128, l_pad):
    b, t, _ = small3.shape
    p_len = 0 if past128 is None else past128.shape[1]
    per_b = lambda r, w: pl.BlockSpec((1, r, w), lambda bi: (bi, 0, 0))
    in_specs = [per_b(t, LANES), _resident((1, LANES))]
    args = [small3, bias128]
    if p_len:
        in_specs.append(per_b(p_len, LANES))
        args.append(past128)
    return pl.pallas_call(
        functools.partial(_fox_scan_kernel, p_len=p_len, t_len=t, l_pad=l_pad), grid=(b,),
        in_specs=in_specs,
        out_specs=[per_b(t, LANES), per_b(l_pad, LANES), per_b(SUBLANES, l_pad)],
        out_shape=[jax.ShapeDtypeStruct((b, t, LANES), F32), jax.ShapeDtypeStruct((b, l_pad, LANES), F32),
                   jax.ShapeDtypeStruct((b, SUBLANES, l_pad), F32)],
        compiler_params=_params("parallel"), name="fox_scan")(*args)


def _gdn_kernel(qkv_ref, z_ref, small_ref, convw_ref, gate_ref, ng_ref, conv0_ref, s0_ref,
                o_ref, sout_ref, prev_sc, s_sc, *, c_len):
    c = pl.program_id(1)
    nc = pl.num_programs(1)

    @pl.when(c == 0)
    def _():
        prev_sc[...] = jnp.zeros(prev_sc.shape, F32)
        prev_sc[c_len - SUBLANES:, :] = conv0_ref[0]
        s_sc[...] = s0_ref[0]

    x = qkv_ref[0]
    prev = prev_sc[...]
    row = lax.broadcasted_iota(jnp.int32, (c_len, 1), 0)
    y = x * convw_ref[CONV_W - 1:CONV_W, :]
    for s in range(1, CONV_W):
        shifted = jnp.where(row < s, pltpu.roll(prev, s, 0), pltpu.roll(x, s, 0))
        y = y + shifted * convw_ref[CONV_W - 1 - s:CONV_W - s, :]
    prev_sc[...] = x
    y = y * _sigmoid(y)

    small = small_ref[0]
    g128 = -jnp.exp(gate_ref[0:1, :]) * _softplus(small + gate_ref[1:2, :])
    beta128 = _sigmoid(small)
    gcum = _cumsum_rows(_tri(c_len), g128)
    gcum_t = gcum.T
    eg128 = jnp.exp(gcum)
    glast128 = gcum[c_len - 1:c_len, :]
    ekd128 = jnp.exp(glast128 - gcum)

    ri = lax.broadcasted_iota(jnp.int32, (c_len, c_len), 0)
    ci = lax.broadcasted_iota(jnp.int32, (c_len, c_len), 1)
    incl = ri >= ci
    strict = ri > ci
    nlev = int(math.log2(c_len))
    assert 2 ** nlev == c_len

    for h in range(GDN_HEADS):
        qh = y[:, h * GDN_DK:(h + 1) * GDN_DK]
        kh = y[:, GDN_QK + h * GDN_DK:GDN_QK + (h + 1) * GDN_DK]
        vh = y[:, 2 * GDN_QK + h * GDN_DV:2 * GDN_QK + (h + 1) * GDN_DV]
        qh = qh * lax.rsqrt(jnp.sum(qh * qh, axis=-1, keepdims=True) + L2_EPS)
        kh = kh * lax.rsqrt(jnp.sum(kh * kh, axis=-1, keepdims=True) + L2_EPS)
        g_col = gcum[:, GA_LANE + h:GA_LANE + h + 1]
        g_row = gcum_t[GA_LANE + h:GA_LANE + h + 1, :]
        eg_col = eg128[:, GA_LANE + h:GA_LANE + h + 1]
        ekd_col = ekd128[:, GA_LANE + h:GA_LANE + h + 1]
        eg_last = eg128[c_len - 1:c_len, GA_LANE + h:GA_LANE + h + 1]
        beta = beta128[:, GB_LANE + h:GB_LANE + h + 1]

        decay = jnp.exp(jnp.where(incl, g_col - g_row, NEG_BIG))
        kb = kh.astype(BF16)
        kk = _dot_nt(kb, kb)
        a_mat = jnp.where(strict, beta * kk * decay, 0.0)
        tm = -a_mat
        pw = a_mat
        for _ in range(1, nlev):
            pb = pw.astype(BF16)
            pw = _dot(pb, pb)
            tm = tm + pw + _dot(tm.astype(BF16), pw.astype(BF16))
        rhs = jnp.concatenate([vh * beta, kh * (beta * eg_col)], -1)
        sol = rhs + _dot(tm.astype(BF16), rhs.astype(BF16))
        u = sol[:, :GDN_DV]
        w = sol[:, GDN_DV:]

        qc = qh * (GDN_DK ** -0.5)
        qk = _dot_nt(qc.astype(BF16), kb) * decay
        s_old = s_sc[h]
        sb = s_old.astype(BF16)
        v_new = u - _dot(w.astype(BF16), sb)
        vb = v_new.astype(BF16)
        o = _dot((qc * eg_col).astype(BF16), sb) + _dot(qk.astype(BF16), vb)
        kd_t = (kh * ekd_col).T.astype(BF16)
        s_sc[h] = s_old * eg_last + _dot(kd_t, vb)
        zh = z_ref[0, :, h * GDN_DV:(h + 1) * GDN_DV]
        o = _rms(o, ng_ref[...]) * (zh * _sigmoid(zh))
        o_ref[0, :, h * GDN_DV:(h + 1) * GDN_DV] = o.astype(o_ref.dtype)

    @pl.when(c == nc - 1)
    def _():
        sout_ref[0] = s_sc[...]


def _gdn(p_gdn3, small3, conv_w, gate2, norm_g, conv0, s0, c_len):
    b, t, _ = p_gdn3.shape
    nc = t // c_len
    return pl.pallas_call(
        functools.partial(_gdn_kernel, c_len=c_len), grid=(b, nc),
        in_specs=[pl.BlockSpec((1, c_len, GDN_CONV_DIM), lambda bi, c: (bi, c, 0)),
                  pl.BlockSpec((1, c_len, GDN_V), lambda bi, c: (bi, c, GDN_CONV_DIM // GDN_V)),
                  pl.BlockSpec((1, c_len, LANES), lambda bi, c: (bi, c, 0)),
                  _resident((CONV_W, GDN_CONV_DIM)), _resident((2, LANES)), _resident((1, GDN_DV)),
                  pl.BlockSpec((1, SUBLANES, GDN_CONV_DIM), lambda bi, c: (bi, 0, 0)),
                  pl.BlockSpec((1, GDN_HEADS, GDN_DK, GDN_DV), lambda bi, c: (bi, 0, 0, 0))],
        out_specs=[pl.BlockSpec((1, c_len, GDN_V), lambda bi, c: (bi, c, 0)),
                   pl.BlockSpec((1, GDN_HEADS, GDN_DK, GDN_DV), lambda bi, c: (bi, 0, 0, 0))],
        out_shape=[jax.ShapeDtypeStruct((b, t, GDN_V), BF16),
                   jax.ShapeDtypeStruct((b, GDN_HEADS, GDN_DK, GDN_DV), F32)],
        scratch_shapes=[pltpu.VMEM((c_len, GDN_CONV_DIM), F32), pltpu.VMEM((GDN_HEADS, GDN_DK, GDN_DV), F32)],
        compiler_params=_params("parallel", "arbitrary"), name="gdn")(
            p_gdn3, p_gdn3, small3, conv_w, gate2, norm_g.reshape(1, -1), conv0, s0)


def _out_ln_kernel(oa_ref, ob_ref, oc_ref, x_ref, w_ref, g_ref, b_ref, wq_ref, x1_ref, qm_ref, *, alpha):
    mix = (_dot(oa_ref[...], w_ref[:MLA_W, :]) + _dot(ob_ref[...], w_ref[MLA_W:MLA_W + GDN_V, :])
           + _dot(oc_ref[...], w_ref[MLA_W + GDN_V:, :]))
    x1 = _layer_norm(alpha * x_ref[...] + mix, g_ref[...], b_ref[...])
    x1_ref[...] = x1
    qm_ref[...] = _dot(x1.astype(BF16), wq_ref[...]).astype(BF16)


def _out_ln(o_a, o_b, o_c, x2d, w_out, g, b, w_xq, alpha):
    m, d = x2d.shape
    tm = min(ROW_TILE, m)
    row = lambda w: pl.BlockSpec((tm, w), lambda i: (i, 0))
    return pl.pallas_call(
        functools.partial(_out_ln_kernel, alpha=alpha), grid=(m // tm,),
        in_specs=[row(MLA_W), row(GDN_V), row(FOX_W), row(d), _resident(w_out.shape), _resident((1, d)),
                  _resident((1, d)), _resident(w_xq.shape)],
        out_specs=[row(d), row(MEM_W)],
        out_shape=[jax.ShapeDtypeStruct((m, d), F32), jax.ShapeDtypeStruct((m, MEM_W), BF16)],
        compiler_params=_params("parallel"), name="out_ln")(
            o_a, o_b, o_c, x2d, w_out, g.reshape(1, d), b.reshape(1, d), w_xq)


def _mem_kv_kernel(m_ref, w_ref, k_ref, v_ref):
    kv = _dot(m_ref[...].astype(BF16), w_ref[...])
    k_ref[...] = kv[:, :MEM_W]
    v_ref[...] = kv[:, MEM_W:]


def _mem_kv(mem2d, w_mkv):
    m, d = mem2d.shape
    tm = min(ROW_TILE, m)
    row = lambda w: pl.BlockSpec((tm, w), lambda i: (i, 0))
    shp = jax.ShapeDtypeStruct((m, MEM_W), F32)
    return pl.pallas_call(
        _mem_kv_kernel, grid=(m // tm,),
        in_specs=[row(d), _resident(w_mkv.shape)], out_specs=[row(MEM_W), row(MEM_W)], out_shape=[shp, shp],
        compiler_params=_params("parallel"), name="mem_kv")(mem2d, w_mkv)


def _mem_attn_kernel(q_ref, k_ref, v_ref, x_ref, w_ref, g_ref, b_ref, o_ref, *, alpha):
    scale = MEM_HD ** -0.5
    outs = []
    for h in range(MEM_HEADS):
        cols = slice(h * MEM_HD, (h + 1) * MEM_HD)
        s = _dot_nt(q_ref[0, :, cols], k_ref[0, :, cols].astype(BF16)) * scale
        e = jnp.exp(s - jnp.max(s, axis=-1, keepdims=True))
        p = e / jnp.sum(e, axis=-1, keepdims=True)
        outs.append(_dot(p.astype(BF16), v_ref[0, :, cols].astype(BF16)))
    o = jnp.concatenate(outs, -1).astype(BF16)
    y = alpha * x_ref[0] + _dot(o, w_ref[...])
    o_ref[0] = _layer_norm(y, g_ref[...], b_ref[...])


def _mem_attn(qm3, mk3, mv3, x3, w_xo, g, b, alpha):
    bsz, t, d = x3.shape
    tq = min(ROW_TILE, t)
    nm = mk3.shape[1]
    return pl.pallas_call(
        functools.partial(_mem_attn_kernel, alpha=alpha), grid=(bsz, t // tq),
        in_specs=[pl.BlockSpec((1, tq, MEM_W), lambda bi, i: (bi, i, 0)),
                  pl.BlockSpec((1, nm, MEM_W), lambda bi, i: (bi, 0, 0)),
                  pl.BlockSpec((1, nm, MEM_W), lambda bi, i: (bi, 0, 0)),
                  pl.BlockSpec((1, tq, d), lambda bi, i: (bi, i, 0)),
                  _resident(w_xo.shape), _resident((1, d)), _resident((1, d))],
        out_specs=pl.BlockSpec((1, tq, d), lambda bi, i: (bi, i, 0)),
        out_shape=jax.ShapeDtypeStruct((bsz, t, d), F32),
        compiler_params=_params("parallel", "parallel"), name="mem_attn")(
            qm3, mk3, mv3, x3, w_xo, g.reshape(1, d), b.reshape(1, d))


FFN_CHUNK = 1024


def _ffn_kernel(x_ref, w1_ref, w2_ref, g_ref, b_ref, o_ref, *, alpha):
    x = x_ref[...]
    xb = x.astype(BF16)
    acc = alpha * x
    for c in range(D_FF // FFN_CHUNK):
        cols = slice(c * FFN_CHUNK, (c + 1) * FFN_CHUNK)
        hdn = jnp.maximum(_dot(xb, w1_ref[:, cols]), 0.0)
        acc = acc + _dot((hdn * hdn).astype(BF16), w2_ref[cols, :])
    o_ref[...] = _layer_norm(acc, g_ref[...], b_ref[...])


def _ffn(x2d, w1, w2, g, b, alpha):
    m, d = x2d.shape
    tm = min(ROW_TILE, m)
    row = pl.BlockSpec((tm, d), lambda i: (i, 0))
    return pl.pallas_call(
        functools.partial(_ffn_kernel, alpha=alpha), grid=(m // tm,),
        in_specs=[row, _resident(w1.shape), _resident(w2.shape), _resident((1, d)), _resident((1, d))],
        out_specs=row, out_shape=jax.ShapeDtypeStruct((m, d), F32),
        compiler_params=_params("parallel"), name="ffn")(x2d, w1, w2, g.reshape(1, d), b.reshape(1, d))


def _round_up(n, m):
    return -(-n // m) * m


def _encoder_layer(x3, past, mem_k, mem_v, lw, rope_tabs, alpha):
    b, t, d = x3.shape
    m = b * t
    x2d = x3.reshape(m, d)
    p_len = 0 if past is None else past[0].shape[1]
    l_len = p_len + t
    l_pad = _round_up(l_len, SCAN_BLOCK)
    cos, sin = rope_tabs

    p_mla, p_gdn, p_small, p_fq, p_fk, p_fv = _in_proj(x2d, lw["w_in"])

    q_a, c_kv, kpe128 = _mla_prep(p_mla, cos, sin, lw["qa_g"], lw["kva_g"], lw["w_uq"], t)
    c_kv3 = c_kv.reshape(b, t, MLA_KV_LORA)
    kpe3 = kpe128.reshape(b, t, LANES)
    if past is None:
        ckv_all, kpe_all = c_kv3, kpe3
    else:
        kpe_past = jnp.pad(past[1], ((0, 0), (0, 0), (ROPE_LO, LANES - ROPE_HI)))
        ckv_all = jnp.concatenate([past[0], c_kv3], 1)
        kpe_all = jnp.concatenate([kpe_past, kpe3], 1)
    if l_pad > l_len:
        ckv_all = jnp.pad(ckv_all, ((0, 0), (0, l_pad - l_len), (0, 0)))
        kpe_all = jnp.pad(kpe_all, ((0, 0), (0, l_pad - l_len), (0, 0)))
    k_a, v_a = _mla_kv(ckv_all.reshape(b * l_pad, MLA_KV_LORA), kpe_all.reshape(b * l_pad, LANES), lw["w_ukv"])
    o_a = _flash(q_a.reshape(b, t, -1), k_a.reshape(b, l_pad, -1), v_a.reshape(b, l_pad, -1),
                 causal="chunk", scale=(MLA_NOPE + MLA_ROPE) ** -0.5, packed_qk=False, kv_len=l_len)

    small3 = p_small.reshape(b, t, LANES)
    p_gdn3 = p_gdn.reshape(b, t, IN_GDN_W)
    if past is None:
        conv_past = jnp.zeros((b, CONV_W - 1, GDN_CONV_DIM), F32)
        s_past = jnp.zeros((b, GDN_HEADS, GDN_DK, GDN_DV), F32)
    else:
        s_past, conv_past = past[5], past[6]
    conv0 = jnp.pad(conv_past, ((0, 0), (SUBLANES - (CONV_W - 1), 0), (0, 0)))
    o_b, s_new = _gdn(p_gdn3, small3, lw["conv_w"], lw["gate"], lw["gdn_g"], conv0, s_past, min(CHUNK, t))
    conv_new = jnp.concatenate([conv_past, p_gdn3[:, -(CONV_W - 1):, :GDN_CONV_DIM]], 1)[:, -(CONV_W - 1):]

    if past is None:
        logf_past, fk_all, fv_all = None, p_fk.reshape(b, t, FOX_W), p_fv.reshape(b, t, FOX_W)
    else:
        logf_past = jnp.pad(past[4], ((0, 0), (0, 0), (FF_LANE, LANES - FF_LANE - FOX_HEADS)))
        fk_all = jnp.concatenate([past[2].reshape(b, p_len, FOX_W), p_fk.reshape(b, t, FOX_W)], 1)
        fv_all = jnp.concatenate([past[3].reshape(b, p_len, FOX_W), p_fv.reshape(b, t, FOX_W)], 1)
    if l_pad > l_len:
        fk_all = jnp.pad(fk_all, ((0, 0), (0, l_pad - l_len), (0, 0)))
        fv_all = jnp.pad(fv_all, ((0, 0), (0, l_pad - l_len), (0, 0)))
    logf128, fcum, fcum_t = _fox_scan(small3, lw["fox_bias"], logf_past, l_pad)
    o_c = _flash(p_fq.reshape(b, t, FOX_W), fk_all, fv_all, causal="frame", scale=FOX_HD ** -0.5,
                 packed_qk=True, kv_len=l_len, fq=fcum, fk=fcum_t)

    x1, qm = _out_ln(o_a.reshape(m, MLA_W), o_b.reshape(m, GDN_V), o_c.reshape(m, FOX_W), x2d, lw["w_out"],
                     lw["ln_g"][0], lw["ln_b"][0], lw["w_xq"], alpha)
    x2 = _mem_attn(qm.reshape(b, t, MEM_W), mem_k, mem_v, x1.reshape(b, t, d), lw["w_xo"],
                   lw["ln_g"][1], lw["ln_b"][1], alpha)
    x3_new = _ffn(x2.reshape(m, d), lw["w_ff1"], lw["w_ff2"], lw["ln_g"][2], lw["ln_b"][2], alpha)

    entries = (c_kv3, kpe3[..., ROPE_LO:ROPE_HI], p_fk.reshape(b, t, FOX_HEADS, FOX_HD),
               p_fv.reshape(b, t, FOX_HEADS, FOX_HD), logf128[..., FF_LANE:FF_LANE + FOX_HEADS], s_new, conv_new)
    return x3_new.reshape(b, t, d), entries


def _lane_block(vals, lane):
    return jnp.pad(vals.astype(F32), ((0, 0), (lane, LANES - lane - vals.shape[-1])))


def kernel(x_prompt, x_sample, cache_mla_ckv, cache_mla_kpe, cache_fox_k, cache_fox_v, cache_fox_logf, state_gdn, state_gdn_conv, cache_mem_k, cache_mem_v, mem_prompt, ln_in_g, ln_in_b, w_in, qa_g, kva_g, w_uq, w_ukv, gdn_conv_w, gdn_a_log, gdn_dt_bias, gdn_norm_g, fox_bf, w_out, w_xq, w_mk, w_mv, w_xo, w_ff1, w_ff2, ln_g, ln_b):
    depth = w_in.shape[0]
    alpha = (2 * depth) ** 0.25
    bp, tp, d = x_prompt.shape
    bs, ts, _ = x_sample.shape
    p_len = cache_mla_ckv.shape[2]

    w_in_p = _pad_w_in(w_in)
    w_uq_p = _pad_w_uq(w_uq)
    w_ukv_p = _pad_w_ukv(w_ukv)
    w_mkv = jnp.concatenate([w_mk, w_mv], -1).astype(BF16)
    gate = jnp.stack([_lane_block(gdn_a_log, GA_LANE), _lane_block(gdn_dt_bias, GA_LANE)], 1)
    fox_bias = _lane_block(fox_bf, FF_LANE)[:, None, :]
    bf = lambda a: a.astype(BF16)
    w_out_b, w_xq_b, w_xo_b, w_ff1_b, w_ff2_b = bf(w_out), bf(w_xq), bf(w_xo), bf(w_ff1), bf(w_ff2)

    tabs_p = _rope_tables(tp, 0)
    tabs_s = _rope_tables(ts, p_len)

    xp = _ln_in(x_prompt.reshape(bp * tp, d), ln_in_g, ln_in_b).reshape(bp, tp, d)
    xs = _ln_in(x_sample.reshape(bs * ts, d), ln_in_g, ln_in_b).reshape(bs, ts, d)
    mem2d = mem_prompt.reshape(bp * N_MEM, d)

    p_new, p_mem, s_new = [], [], []
    for l in range(depth):
        lw = dict(w_in=w_in_p[l], qa_g=qa_g[l], kva_g=kva_g[l], w_uq=w_uq_p[l], w_ukv=w_ukv_p[l],
                  conv_w=gdn_conv_w[l], gate=gate[l], gdn_g=gdn_norm_g[l], fox_bias=fox_bias[l],
                  w_out=w_out_b[l], w_xq=w_xq_b[l], w_xo=w_xo_b[l], w_ff1=w_ff1_b[l], w_ff2=w_ff2_b[l],
                  ln_g=ln_g[l], ln_b=ln_b[l])
        mk, mv = _mem_kv(mem2d, w_mkv[l])
        mk3 = mk.reshape(bp, N_MEM, MEM_W)
        mv3 = mv.reshape(bp, N_MEM, MEM_W)
        xp, ent_p = _encoder_layer(xp, None, mk3, mv3, lw, tabs_p, alpha)
        p_new.append(ent_p)
        p_mem.append((mk.reshape(bp, N_MEM, MEM_HEADS, MEM_HD), mv.reshape(bp, N_MEM, MEM_HEADS, MEM_HD)))
        past = (cache_mla_ckv[l], cache_mla_kpe[l], cache_fox_k[l], cache_fox_v[l], cache_fox_logf[l],
                state_gdn[l], state_gdn_conv[l])
        xs, ent_s = _encoder_layer(xs, past, cache_mem_k[l].reshape(bs, N_MEM, MEM_W),
                                   cache_mem_v[l].reshape(bs, N_MEM, MEM_W), lw, tabs_s, alpha)
        s_new.append(ent_s)

    stack = lambda entries, i: jnp.stack([e[i] for e in entries])
    return (xp, xs,
            stack(p_new, 0), stack(p_new, 1), stack(p_new, 2), stack(p_new, 3), stack(p_new, 4),
            stack(p_new, 5), stack(p_new, 6), stack(p_mem, 0), stack(p_mem, 1),
            stack(s_new, 0), stack(s_new, 1), stack(s_new, 2), stack(s_new, 3), stack(s_new, 4),
            stack(s_new, 5), stack(s_new, 6))
```

```python
import functools
import math

import jax
import jax.numpy as jnp
from jax import lax
from jax.experimental import pallas as pl
from jax.experimental.pallas import tpu as pltpu

F32 = jnp.float32
BF16 = jnp.bfloat16

D_MODEL = 1024
CHUNK = 64
MLA_HEADS = 4
MLA_NOPE = 64
MLA_ROPE = 32
MLA_V = 64
MLA_Q_LORA = 384
MLA_KV_LORA = 256
ROPE_THETA = 10000.0
GDN_HEADS = 4
GDN_DK = 128
GDN_DV = 128
CONV_W = 4
FOX_HEADS = 4
FOX_HD = 64
N_MEM = 256
MEM_HEADS = 4
MEM_HD = 128
D_FF = 4 * D_MODEL
LN_EPS = 1e-5
RMS_EPS = 1e-6
L2_EPS = 1e-6

GDN_QK = GDN_HEADS * GDN_DK
GDN_V = GDN_HEADS * GDN_DV
GDN_CONV_DIM = 2 * GDN_QK + GDN_V
FOX_W = FOX_HEADS * FOX_HD
MLA_W = MLA_HEADS * MLA_V
MEM_W = MEM_HEADS * MEM_HD

LANES = 128
SUBLANES = 8
ROW_TILE = 512
ATTN_BLOCK = 512
SCAN_BLOCK = 256
VMEM_LIMIT = 56 * 1024 * 1024
NEG_BIG = -1e30
LOG2E = 1.0 / math.log(2.0)

FF_LANE = 0
GA_LANE = 4
GB_LANE = 8
ROPE_LO = MLA_NOPE
ROPE_HI = MLA_NOPE + MLA_ROPE


def _params(*sem):
    return pltpu.CompilerParams(dimension_semantics=sem, vmem_limit_bytes=VMEM_LIMIT)


def _resident(shape):
    nd = len(shape)
    return pl.BlockSpec(shape, lambda *_: (0,) * nd)


def _layer_norm(y, g, b):
    mu = jnp.mean(y, axis=-1, keepdims=True)
    yc = y - mu
    var = jnp.mean(yc * yc, axis=-1, keepdims=True)
    return yc * lax.rsqrt(var + LN_EPS) * g + b


def _sigmoid(x):
    return 1.0 / (1.0 + jnp.exp(-x))


def _softplus(x):
    return jnp.maximum(x, 0.0) + jnp.log1p(jnp.exp(-jnp.abs(x)))


def _dot(a, b):
    return jnp.dot(a, b, preferred_element_type=F32)


def _dot_nt(a, b):
    return lax.dot_general(a, b, (((1,), (1,)), ((), ())), preferred_element_type=F32)


def _split3(x):
    hi = x.astype(BF16)
    r1 = x - hi.astype(F32)
    mid = r1.astype(BF16)
    lo = (r1 - mid.astype(F32)).astype(BF16)
    return hi, mid, lo


def _cumsum_rows(tri, x):
    hi, mid, lo = _split3(x)
    return _dot(tri, hi) + _dot(tri, mid) + _dot(tri, lo)


def _tri(n):
    r = lax.broadcasted_iota(jnp.int32, (n, n), 0)
    c = lax.broadcasted_iota(jnp.int32, (n, n), 1)
    return (r >= c).astype(BF16)


def _rope_table_kernel(cos_ref, sin_ref, *, start):
    shape = cos_ref.shape
    lane = lax.broadcasted_iota(jnp.int32, shape, 1)
    row = lax.broadcasted_iota(jnp.int32, shape, 0)
    half = MLA_ROPE // 2
    idx = ((lane - ROPE_LO) & (half - 1)).astype(F32)
    inv = jnp.exp(idx * (-(2.0 / MLA_ROPE) * math.log(ROPE_THETA)))
    ang = (row + start).astype(F32) * inv
    in_rope = (lane >= ROPE_LO) & (lane < ROPE_HI)
    cos_ref[...] = jnp.where(in_rope, jnp.cos(ang), 1.0)
    sin_ref[...] = jnp.where(in_rope, jnp.sin(ang), 0.0)


def _rope_tables(t, start):
    shp = jax.ShapeDtypeStruct((t, LANES), F32)
    return pl.pallas_call(functools.partial(_rope_table_kernel, start=start),
                          out_shape=(shp, shp), name="rope_tables")()


def _ln_kernel(x_ref, g_ref, b_ref, o_ref):
    o_ref[...] = _layer_norm(x_ref[...], g_ref[...], b_ref[...])


def _ln_in(x2d, g, b):
    m, d = x2d.shape
    tm = min(ROW_TILE, m)
    return pl.pallas_call(
        _ln_kernel, grid=(m // tm,),
        in_specs=[pl.BlockSpec((tm, d), lambda i: (i, 0)), _resident((1, d)), _resident((1, d))],
        out_specs=pl.BlockSpec((tm, d), lambda i: (i, 0)),
        out_shape=jax.ShapeDtypeStruct((m, d), F32),
        compiler_params=_params("parallel"), name="ln_in")(x2d, g.reshape(1, d), b.reshape(1, d))


IN_MLA_W = MLA_Q_LORA + MLA_KV_LORA + 2 * LANES
IN_GDN_W = GDN_CONV_DIM + GDN_V
IN_PAD_COLS = (IN_MLA_W, IN_GDN_W, LANES, FOX_W, FOX_W, FOX_W)
IN_PAD_OFFS = tuple(sum(IN_PAD_COLS[:n]) for n in range(len(IN_PAD_COLS) + 1))


def _in_proj_kernel(x_ref, w_ref, *o_refs):
    xb = x_ref[...].astype(BF16)
    for o_ref, lo, hi in zip(o_refs, IN_PAD_OFFS[:-1], IN_PAD_OFFS[1:]):
        o_ref[...] = _dot(xb, w_ref[:, lo:hi])


def _in_proj(x2d, w_pad):
    m, d = x2d.shape
    tm = min(ROW_TILE, m)
    return pl.pallas_call(
        _in_proj_kernel, grid=(m // tm,),
        in_specs=[pl.BlockSpec((tm, d), lambda i: (i, 0)), _resident(w_pad.shape)],
        out_specs=[pl.BlockSpec((tm, w), lambda i: (i, 0)) for w in IN_PAD_COLS],
        out_shape=[jax.ShapeDtypeStruct((m, w), F32) for w in IN_PAD_COLS],
        compiler_params=_params("parallel"), name="in_proj")(x2d, w_pad)


def _pad_w_in(w_in):
    offs = [0]
    for s in (MLA_Q_LORA, MLA_KV_LORA, MLA_ROPE, GDN_QK, GDN_QK, GDN_V, GDN_V, GDN_HEADS, GDN_HEADS,
              FOX_W, FOX_W, FOX_W, FOX_HEADS):
        offs.append(offs[-1] + s)
    cq, ckv, kpe, gq, gk, gv, gz, ga, gb, fq, fk, fv, ff = [w_in[..., a:b] for a, b in zip(offs[:-1], offs[1:])]
    half = MLA_ROPE // 2
    kpe_rot = jnp.concatenate([-kpe[..., half:], kpe[..., :half]], -1)

    def z(n):
        return jnp.zeros(w_in.shape[:-1] + (n,), w_in.dtype)

    def rope_block(w):
        return jnp.concatenate([z(ROPE_LO), w, z(LANES - ROPE_HI)], -1)

    small = jnp.concatenate([ff, ga, gb, z(LANES - 3 * GDN_HEADS)], -1)
    return jnp.concatenate([cq, ckv, rope_block(kpe), rope_block(kpe_rot), gq, gk, gv, gz, small, fq, fk, fv],
                           -1).astype(BF16)


def _rms(x, g):
    return x * lax.rsqrt(jnp.mean(x * x, axis=-1, keepdims=True) + RMS_EPS) * g


def _mla_prep_kernel(p_ref, cos_ref, sin_ref, qg_ref, kg_ref, wq_ref, q_ref, ckv_ref, kpe_ref):
    cos = cos_ref[...]
    sin = sin_ref[...]
    c_q = _rms(p_ref[:, :MLA_Q_LORA], qg_ref[...]).astype(BF16)
    qq = _dot(c_q, wq_ref[...])
    hw = MLA_HEADS * LANES
    for h in range(MLA_HEADS):
        a = qq[:, h * LANES:(h + 1) * LANES]
        b = qq[:, hw + h * LANES:hw + (h + 1) * LANES]
        q_ref[:, h * LANES:(h + 1) * LANES] = (a * cos + b * sin).astype(BF16)
    lo = MLA_Q_LORA
    ckv_ref[...] = _rms(p_ref[:, lo:lo + MLA_KV_LORA], kg_ref[...])
    lo += MLA_KV_LORA
    kpe_ref[...] = p_ref[:, lo:lo + LANES] * cos + p_ref[:, lo + LANES:lo + 2 * LANES] * sin


def _mla_prep(p_mla, cos, sin, qa_g, kva_g, w_uq2, t):
    m = p_mla.shape[0]
    tm = min(ROW_TILE, t)
    nt = t // tm
    row = lambda w: pl.BlockSpec((tm, w), lambda i: (i, 0))
    tab = pl.BlockSpec((tm, LANES), lambda i: (i % nt, 0))
    return pl.pallas_call(
        _mla_prep_kernel, grid=(m // tm,),
        in_specs=[row(IN_MLA_W), tab, tab, _resident((1, MLA_Q_LORA)), _resident((1, MLA_KV_LORA)),
                  _resident(w_uq2.shape)],
        out_specs=[row(MLA_HEADS * LANES), row(MLA_KV_LORA), row(LANES)],
        out_shape=[jax.ShapeDtypeStruct((m, MLA_HEADS * LANES), BF16),
                   jax.ShapeDtypeStruct((m, MLA_KV_LORA), F32),
                   jax.ShapeDtypeStruct((m, LANES), F32)],
        compiler_params=_params("parallel"), name="mla_prep")(
            p_mla, cos, sin, qa_g.reshape(1, -1), kva_g.reshape(1, -1), w_uq2)


def _pad_w_uq(w_uq):
    half = MLA_ROPE // 2
    w = w_uq.reshape(w_uq.shape[:-1] + (MLA_HEADS, MLA_NOPE + MLA_ROPE))
    nope, rp = w[..., :MLA_NOPE], w[..., MLA_NOPE:]
    rot = jnp.concatenate([-rp[..., half:], rp[..., :half]], -1)
    zpad = jnp.zeros(w.shape[:-1] + (LANES - ROPE_HI,), w.dtype)
    plain = jnp.concatenate([nope, rp, zpad], -1)
    rotated = jnp.concatenate([jnp.zeros_like(nope), rot, zpad], -1)
    flat = lambda a: a.reshape(a.shape[:-2] + (MLA_HEADS * LANES,))
    return jnp.concatenate([flat(plain), flat(rotated)], -1).astype(BF16)


def _mla_kv_kernel(ckv_ref, kpe_ref, w_ref, k_ref, v_ref):
    kv = _dot(ckv_ref[...].astype(BF16), w_ref[...])
    kpe = kpe_ref[...]
    kw = MLA_HEADS * LANES
    for h in range(MLA_HEADS):
        k_ref[:, h * LANES:(h + 1) * LANES] = (kv[:, h * LANES:(h + 1) * LANES] + kpe).astype(BF16)
    v_ref[...] = kv[:, kw:].astype(BF16)


def _mla_kv(ckv_all, kpe_all, w_ukv2):
    r = ckv_all.shape[0]
    tm = min(ROW_TILE, r)
    row = lambda w: pl.BlockSpec((tm, w), lambda i: (i, 0))
    return pl.pallas_call(
        _mla_kv_kernel, grid=(r // tm,),
        in_specs=[row(MLA_KV_LORA), row(LANES), _resident(w_ukv2.shape)],
        out_specs=[row(MLA_HEADS * LANES), row(MLA_W)],
        out_shape=[jax.ShapeDtypeStruct((r, MLA_HEADS * LANES), BF16), jax.ShapeDtypeStruct((r, MLA_W), BF16)],
        compiler_params=_params("parallel"), name="mla_kv")(ckv_all, kpe_all, w_ukv2)


def _pad_w_ukv(w_ukv):
    w = w_ukv.reshape(w_ukv.shape[:-1] + (MLA_HEADS, MLA_NOPE + MLA_V))
    nope, v = w[..., :MLA_NOPE], w[..., MLA_NOPE:]
    kpad = jnp.concatenate([nope, jnp.zeros(nope.shape[:-1] + (LANES - MLA_NOPE,), w.dtype)], -1)
    flat = lambda a: a.reshape(a.shape[:-2] + (-1,))
    return jnp.concatenate([flat(kpad), flat(v)], -1).astype(BF16)


def _flash_kernel(*refs, tq, tk, q_start, kv_len, kv_pad, causal, scale, packed_qk, has_f, heads):
    if has_f:
        q_ref, k_ref, v_ref, fq_ref, fk_ref, o_ref, m_sc, l_sc, acc_sc = refs
    else:
        q_ref, k_ref, v_ref, o_ref, m_sc, l_sc, acc_sc = refs
    i = pl.program_id(1)
    j = pl.program_id(2)
    nk = pl.num_programs(2)

    @pl.when(j == 0)
    def _():
        m_sc[...] = jnp.full(m_sc.shape, NEG_BIG, F32)
        l_sc[...] = jnp.zeros(l_sc.shape, F32)
        acc_sc[...] = jnp.zeros(acc_sc.shape, F32)

    qmin = q_start + i * tq
    qmax = qmin + tq - 1
    kmin = j * tk
    kmax = kmin + tk - 1
    if causal == "frame":
        any_vis = kmin <= qmax
        all_vis = kmax <= qmin
    else:
        any_vis = (kmin // CHUNK) <= (qmax // CHUNK)
        all_vis = (kmax // CHUNK) <= (qmin // CHUNK)
    if kv_len < kv_pad:
        all_vis = jnp.logical_and(all_vis, kmax < kv_len)

    fold = math.log2(scale) == round(math.log2(scale))
    half_lane = lax.broadcasted_iota(jnp.int32, (1, LANES), 1) < (LANES // 2)

    def body(masked):
        if masked:
            qpos = qmin + lax.broadcasted_iota(jnp.int32, (tq, tk), 0)
            kpos = kmin + lax.broadcasted_iota(jnp.int32, (tq, tk), 1)
            if causal == "frame":
                vis = kpos <= qpos
            else:
                vis = (kpos // CHUNK) <= (qpos // CHUNK)
            if kv_len < kv_pad:
                vis = jnp.logical_and(vis, kpos < kv_len)
        hs = range(heads)
        mine = [half_lane if h % 2 == 0 else jnp.logical_not(half_lane) for h in hs]
        s_l = []
        for h in hs:
            blk = slice((h // 2) * LANES, (h // 2 + 1) * LANES) if packed_qk else slice(h * LANES, (h + 1) * LANES)
            qh = q_ref[0, :, blk]
            if packed_qk:
                qh = jnp.where(mine[h], qh, 0.0)
            if fold:
                qh = qh * scale
            s = _dot_nt(qh.astype(BF16), k_ref[0, :, blk].astype(BF16))
            if has_f:
                s = s + (fq_ref[0, :, FF_LANE + h:FF_LANE + h + 1] - fk_ref[0, FF_LANE + h:FF_LANE + h + 1, :])
            if masked:
                s = jnp.where(vis, s, NEG_BIG)
            s_l.append(s)
        c = LOG2E if fold else LOG2E * scale
        m_old = [m_sc[h] for h in hs]
        m_new = [jnp.maximum(m_old[h], jnp.max(s_l[h], axis=-1, keepdims=True)) for h in hs]
        alpha = [jnp.exp2((m_old[h] - m_new[h]) * c) for h in hs]
        e_l = [jnp.exp2((s_l[h] - m_new[h]) * c) for h in hs]
        for h in hs:
            l_sc[h] = alpha[h] * l_sc[h] + jnp.sum(e_l[h], axis=-1, keepdims=True)
            m_sc[h] = m_new[h]
        for pair in range(heads // 2):
            cols = slice(pair * LANES, (pair + 1) * LANES)
            vpair = v_ref[0, :, cols].astype(BF16)
            pv = None
            for h in (2 * pair, 2 * pair + 1):
                d = _dot(e_l[h].astype(BF16), jnp.where(mine[h], vpair, jnp.zeros_like(vpair)))
                pv = d if pv is None else pv + d
            a_sel = jnp.where(half_lane, alpha[2 * pair], alpha[2 * pair + 1])
            acc_sc[:, cols] = acc_sc[:, cols] * a_sel + pv

    @pl.when(all_vis)
    def _():
        body(False)

    @pl.when(jnp.logical_and(any_vis, jnp.logical_not(all_vis)))
    def _():
        body(True)

    @pl.when(j == nk - 1)
    def _():
        for pair in range(heads // 2):
            inv = jnp.where(half_lane, 1.0 / l_sc[2 * pair], 1.0 / l_sc[2 * pair + 1])
            cols = slice(pair * LANES, (pair + 1) * LANES)
            o_ref[0, :, cols] = (acc_sc[:, cols] * inv).astype(o_ref.dtype)


def _flash(q, k, v, *, causal, scale, packed_qk, kv_len, fq=None, fk=None, tq=None, tk=None):
    b, t, wq = q.shape
    lp = k.shape[1]
    heads = v.shape[2] // (LANES // 2)
    tq = tq or min(ATTN_BLOCK, t)
    tk = tk or (min(ATTN_BLOCK, lp) if t >= ATTN_BLOCK else lp)
    q_start = kv_len - t
    nq, nk = t // tq, lp // tk

    def last_block(i):
        qmax = q_start + i * tq + tq - 1
        last = qmax if causal == "frame" else (qmax // CHUNK) * CHUNK + CHUNK - 1
        return jnp.minimum(last // tk, nk - 1)

    kv_map = lambda bi, i, j: (bi, jnp.minimum(j, last_block(i)), 0)
    in_specs = [pl.BlockSpec((1, tq, wq), lambda bi, i, j: (bi, i, 0)),
                pl.BlockSpec((1, tk, wq), kv_map),
                pl.BlockSpec((1, tk, v.shape[2]), kv_map)]
    args = [q, k, v]
    has_f = fq is not None
    if has_f:
        in_specs += [pl.BlockSpec((1, tq, LANES), lambda bi, i, j: (bi, i + q_start // tq, 0)),
                     pl.BlockSpec((1, SUBLANES, tk), lambda bi, i, j: (bi, 0, jnp.minimum(j, last_block(i))))]
        args += [fq, fk]
        assert q_start % tq == 0
    kern = functools.partial(_flash_kernel, tq=tq, tk=tk, q_start=q_start, kv_len=kv_len, kv_pad=lp,
                             causal=causal, scale=scale, packed_qk=packed_qk, has_f=has_f, heads=heads)
    return pl.pallas_call(
        kern, grid=(b, nq, nk), in_specs=in_specs,
        out_specs=pl.BlockSpec((1, tq, v.shape[2]), lambda bi, i, j: (bi, i, 0)),
        out_shape=jax.ShapeDtypeStruct((b, t, v.shape[2]), BF16),
        scratch_shapes=[pltpu.VMEM((heads, tq, 1), F32), pltpu.VMEM((heads, tq, 1), F32),
                        pltpu.VMEM((tq, v.shape[2]), F32)],
        compiler_params=_params("parallel", "parallel", "arbitrary"),
        name="flash_" + causal)(*args)


def _fox_scan_kernel(*refs, p_len, t_len, l_pad):
    if p_len:
        small_ref, bias_ref, past_ref, logf_ref, fc_ref, fct_ref = refs
    else:
        small_ref, bias_ref, logf_ref, fc_ref, fct_ref = refs
    x = small_ref[0] + bias_ref[...]
    logf = jnp.minimum(x, 0.0) - jnp.log1p(jnp.exp(-jnp.abs(x)))
    logf_ref[0] = logf
    bs = SCAN_BLOCK
    tri = _tri(bs)
    carry = jnp.zeros((1, LANES), F32)
    for blk in range(l_pad // bs):
        lo = blk * bs
        if lo + bs <= p_len:
            xb = past_ref[0, lo:lo + bs, :]
        elif lo >= p_len and lo + bs <= p_len + t_len:
            xb = logf[lo - p_len:lo - p_len + bs]
        else:
            assert lo == p_len and t_len < bs
            xb = jnp.concatenate([logf, jnp.zeros((bs - t_len, LANES), F32)], 0)
        cs = _cumsum_rows(tri, xb) + carry
        carry = cs[bs - 1:bs, :]
        fc_ref[0, lo:lo + bs, :] = cs
        fct_ref[0, :, lo:lo + bs] = cs.T[:SUBLANES, :]


def _fox_scan(small3, bias128, past128, l_pad):
    b, t, _ = small3.shape
    p_len = 0 if past128 is None else past128.shape[1]
    per_b = lambda r, w: pl.BlockSpec((1, r, w), lambda bi: (bi, 0, 0))
    in_specs = [per_b(t, LANES), _resident((1, LANES))]
    args = [small3, bias128]
    if p_len:
        in_specs.append(per_b(p_len, LANES))
        args.append(past128)
    return pl.pallas_call(
        functools.partial(_fox_scan_kernel, p_len=p_len, t_len=t, l_pad=l_pad), grid=(b,),
        in_specs=in_specs,
        out_specs=[per_b(t, LANES), per_b(l_pad, LANES), per_b(SUBLANES, l_pad)],
        out_shape=[jax.ShapeDtypeStruct((b, t, LANES), F32), jax.ShapeDtypeStruct((b, l_pad, LANES), F32),
                   jax.ShapeDtypeStruct((b, SUBLANES, l_pad), F32)],
        compiler_params=_params("parallel"), name="fox_scan")(*args)


GDN_BLOCK = 256


def _gdn_kernel(qkv_ref, z_ref, small_ref, convw_ref, gate_ref, ng_ref, conv0_ref, s0_ref,
                o_ref, sout_ref, prev_sc, s_sc, *, blk, c_len):
    step = pl.program_id(1)
    nsteps = pl.num_programs(1)
    nsub = blk // c_len

    @pl.when(step == 0)
    def _():
        prev_sc[...] = jnp.zeros(prev_sc.shape, F32)
        prev_sc[blk - SUBLANES:, :] = conv0_ref[0]
        s_sc[...] = s0_ref[0]

    x = qkv_ref[0]
    prev = prev_sc[...]
    row = lax.broadcasted_iota(jnp.int32, (blk, 1), 0)
    y = x * convw_ref[CONV_W - 1:CONV_W, :]
    for s in range(1, CONV_W):
        shifted = jnp.where(row < s, pltpu.roll(prev, s, 0), pltpu.roll(x, s, 0))
        y = y + shifted * convw_ref[CONV_W - 1 - s:CONV_W - s, :]
    prev_sc[...] = x
    y = y * _sigmoid(y)

    ri = lax.broadcasted_iota(jnp.int32, (blk, blk), 0)
    ci = lax.broadcasted_iota(jnp.int32, (blk, blk), 1)
    same = (ri // c_len) == (ci // c_len)
    incl = jnp.logical_and(ri >= ci, same)
    strict = jnp.logical_and(ri > ci, same)

    small = small_ref[0]
    g128 = -jnp.exp(gate_ref[0:1, :]) * _softplus(small + gate_ref[1:2, :])
    beta128 = _sigmoid(small)
    gcum = _cumsum_rows(incl.astype(BF16), g128)
    gcum_t = gcum.T
    eg128 = jnp.exp(gcum)
    glast128 = jnp.concatenate(
        [jnp.broadcast_to(gcum[(c + 1) * c_len - 1:(c + 1) * c_len, :], (c_len, LANES)) for c in range(nsub)], 0)
    ekd128 = jnp.exp(glast128 - gcum)
    nlev = int(math.log2(c_len))
    assert 2 ** nlev == c_len

    heads = range(GDN_HEADS)
    kh_l, kb_l, a_l, decay_l, rhs_l, qcb_l, qd_l, kdt_l = [], [], [], [], [], [], [], []
    for h in heads:
        qh = y[:, h * GDN_DK:(h + 1) * GDN_DK]
        kh = y[:, GDN_QK + h * GDN_DK:GDN_QK + (h + 1) * GDN_DK]
        vh = y[:, 2 * GDN_QK + h * GDN_DV:2 * GDN_QK + (h + 1) * GDN_DV]
        qh = qh * lax.rsqrt(jnp.sum(qh * qh, axis=-1, keepdims=True) + L2_EPS)
        kh = kh * lax.rsqrt(jnp.sum(kh * kh, axis=-1, keepdims=True) + L2_EPS)
        lane = GA_LANE + h
        g_col = gcum[:, lane:lane + 1]
        g_row = gcum_t[lane:lane + 1, :]
        eg_col = eg128[:, lane:lane + 1]
        beta = beta128[:, GB_LANE + h:GB_LANE + h + 1]
        decay = jnp.exp(jnp.where(incl, g_col - g_row, NEG_BIG))
        kb = kh.astype(BF16)
        a_l.append(jnp.where(strict, beta * _dot_nt(kb, kb) * decay, 0.0))
        rhs_l.append(jnp.concatenate([vh * beta, kh * (beta * eg_col)], -1))
        qc = qh * (GDN_DK ** -0.5)
        qcb_l.append(qc.astype(BF16))
        qd_l.append((qc * eg_col).astype(BF16))
        kdt_l.append((kh * ekd128[:, lane:lane + 1]).T.astype(BF16))
        kb_l.append(kb)
        decay_l.append(decay)

    tm_l = [-a for a in a_l]
    pw_l = a_l
    for _ in range(1, nlev):
        pb_l = [pw.astype(BF16) for pw in pw_l]
        pw_l = [_dot(pb, pb) for pb in pb_l]
        tm_l = [tm + pw + _dot(tm.astype(BF16), pw.astype(BF16)) for tm, pw in zip(tm_l, pw_l)]
    sol_l = [rhs + _dot(tm.astype(BF16), rhs.astype(BF16)) for tm, rhs in zip(tm_l, rhs_l)]
    u_l = [sol[:, :GDN_DV] for sol in sol_l]
    wb_l = [sol[:, GDN_DV:].astype(BF16) for sol in sol_l]
    qk_l = [(_dot_nt(qcb, kb) * decay).astype(BF16) for qcb, kb, decay in zip(qcb_l, kb_l, decay_l)]

    s_l = [s_sc[h] for h in heads]
    v_parts = [[] for _ in heads]
    qs_parts = [[] for _ in heads]
    for c in range(nsub):
        rows = slice(c * c_len, (c + 1) * c_len)
        r_l = [_dot(jnp.concatenate([wb_l[h][rows], qd_l[h][rows]], 0), s_l[h].astype(BF16)) for h in heads]
        for h in heads:
            v_parts[h].append(u_l[h][rows] - r_l[h][:c_len])
            qs_parts[h].append(r_l[h][c_len:])
        s_l = [s_l[h] * eg128[(c + 1) * c_len - 1:(c + 1) * c_len, GA_LANE + h:GA_LANE + h + 1]
               + _dot(kdt_l[h][:, rows], v_parts[h][c].astype(BF16)) for h in heads]
    for h in heads:
        s_sc[h] = s_l[h]
        v_all = v_parts[h][0] if nsub == 1 else jnp.concatenate(v_parts[h], 0)
        qs_all = qs_parts[h][0] if nsub == 1 else jnp.concatenate(qs_parts[h], 0)
        o = qs_all + _dot(qk_l[h], v_all.astype(BF16))
        zh = z_ref[0, :, h * GDN_DV:(h + 1) * GDN_DV]
        o = _rms(o, ng_ref[...]) * (zh * _sigmoid(zh))
        o_ref[0, :, h * GDN_DV:(h + 1) * GDN_DV] = o.astype(o_ref.dtype)

    @pl.when(step == nsteps - 1)
    def _():
        sout_ref[0] = s_sc[...]


def _gdn(p_gdn3, small3, conv_w, gate2, norm_g, conv0, s0, c_len):
    b, t, _ = p_gdn3.shape
    blk = min(GDN_BLOCK, t)
    return pl.pallas_call(
        functools.partial(_gdn_kernel, blk=blk, c_len=c_len), grid=(b, t // blk),
        in_specs=[pl.BlockSpec((1, blk, GDN_CONV_DIM), lambda bi, c: (bi, c, 0)),
                  pl.BlockSpec((1, blk, GDN_V), lambda bi, c: (bi, c, GDN_CONV_DIM // GDN_V)),
                  pl.BlockSpec((1, blk, LANES), lambda bi, c: (bi, c, 0)),
                  _resident((CONV_W, GDN_CONV_DIM)), _resident((2, LANES)), _resident((1, GDN_DV)),
                  pl.BlockSpec((1, SUBLANES, GDN_CONV_DIM), lambda bi, c: (bi, 0, 0)),
                  pl.BlockSpec((1, GDN_HEADS, GDN_DK, GDN_DV), lambda bi, c: (bi, 0, 0, 0))],
        out_specs=[pl.BlockSpec((1, blk, GDN_V), lambda bi, c: (bi, c, 0)),
                   pl.BlockSpec((1, GDN_HEADS, GDN_DK, GDN_DV), lambda bi, c: (bi, 0, 0, 0))],
        out_shape=[jax.ShapeDtypeStruct((b, t, GDN_V), BF16),
                   jax.ShapeDtypeStruct((b, GDN_HEADS, GDN_DK, GDN_DV), F32)],
        scratch_shapes=[pltpu.VMEM((blk, GDN_CONV_DIM), F32), pltpu.VMEM((GDN_HEADS, GDN_DK, GDN_DV), F32)],
        compiler_params=_params("parallel", "arbitrary"), name="gdn")(
            p_gdn3, p_gdn3, small3, conv_w, gate2, norm_g.reshape(1, -1), conv0, s0)


def _out_ln_kernel(oa_ref, ob_ref, oc_ref, x_ref, w_ref, g_ref, b_ref, wq_ref, x1_ref, qm_ref, *, alpha):
    mix = (_dot(oa_ref[...], w_ref[:MLA_W, :]) + _dot(ob_ref[...], w_ref[MLA_W:MLA_W + GDN_V, :])
           + _dot(oc_ref[...], w_ref[MLA_W + GDN_V:, :]))
    x1 = _layer_norm(alpha * x_ref[...] + mix, g_ref[...], b_ref[...])
    x1_ref[...] = x1
    qm_ref[...] = _dot(x1.astype(BF16), wq_ref[...]).astype(BF16)


def _out_ln(o_a, o_b, o_c, x2d, w_out, g, b, w_xq, alpha):
    m, d = x2d.shape
    tm = min(ROW_TILE, m)
    row = lambda w: pl.BlockSpec((tm, w), lambda i: (i, 0))
    return pl.pallas_call(
        functools.partial(_out_ln_kernel, alpha=alpha), grid=(m // tm,),
        in_specs=[row(MLA_W), row(GDN_V), row(FOX_W), row(d), _resident(w_out.shape), _resident((1, d)),
                  _resident((1, d)), _resident(w_xq.shape)],
        out_specs=[row(d), row(MEM_W)],
        out_shape=[jax.ShapeDtypeStruct((m, d), F32), jax.ShapeDtypeStruct((m, MEM_W), BF16)],
        compiler_params=_params("parallel"), name="out_ln")(
            o_a, o_b, o_c, x2d, w_out, g.reshape(1, d), b.reshape(1, d), w_xq)


def _mem_kv_kernel(m_ref, w_ref, k_ref, v_ref):
    kv = _dot(m_ref[...].astype(BF16), w_ref[...])
    k_ref[...] = kv[:, :MEM_W]
    v_ref[...] = kv[:, MEM_W:]


def _mem_kv(mem2d, w_mkv):
    m, d = mem2d.shape
    tm = min(ROW_TILE, m)
    row = lambda w: pl.BlockSpec((tm, w), lambda i: (i, 0))
    shp = jax.ShapeDtypeStruct((m, MEM_W), F32)
    return pl.pallas_call(
        _mem_kv_kernel, grid=(m // tm,),
        in_specs=[row(d), _resident(w_mkv.shape)], out_specs=[row(MEM_W), row(MEM_W)], out_shape=[shp, shp],
        compiler_params=_params("parallel"), name="mem_kv")(mem2d, w_mkv)


def _mem_attn_kernel(q_ref, k_ref, v_ref, x_ref, w_ref, g_ref, b_ref, o_ref, *, alpha):
    scale = MEM_HD ** -0.5
    outs = []
    for h in range(MEM_HEADS):
        cols = slice(h * MEM_HD, (h + 1) * MEM_HD)
        s = _dot_nt(q_ref[0, :, cols], k_ref[0, :, cols].astype(BF16)) * scale
        e = jnp.exp(s - jnp.max(s, axis=-1, keepdims=True))
        p = e / jnp.sum(e, axis=-1, keepdims=True)
        outs.append(_dot(p.astype(BF16), v_ref[0, :, cols].astype(BF16)))
    o = jnp.concatenate(outs, -1).astype(BF16)
    y = alpha * x_ref[0] + _dot(o, w_ref[...])
    o_ref[0] = _layer_norm(y, g_ref[...], b_ref[...])


def _mem_attn(qm3, mk3, mv3, x3, w_xo, g, b, alpha):
    bsz, t, d = x3.shape
    tq = min(ROW_TILE, t)
    nm = mk3.shape[1]
    return pl.pallas_call(
        functools.partial(_mem_attn_kernel, alpha=alpha), grid=(bsz, t // tq),
        in_specs=[pl.BlockSpec((1, tq, MEM_W), lambda bi, i: (bi, i, 0)),
                  pl.BlockSpec((1, nm, MEM_W), lambda bi, i: (bi, 0, 0)),
                  pl.BlockSpec((1, nm, MEM_W), lambda bi, i: (bi, 0, 0)),
                  pl.BlockSpec((1, tq, d), lambda bi, i: (bi, i, 0)),
                  _resident(w_xo.shape), _resident((1, d)), _resident((1, d))],
        out_specs=pl.BlockSpec((1, tq, d), lambda bi, i: (bi, i, 0)),
        out_shape=jax.ShapeDtypeStruct((bsz, t, d), F32),
        compiler_params=_params("parallel", "parallel"), name="mem_attn")(
            qm3, mk3, mv3, x3, w_xo, g.reshape(1, d), b.reshape(1, d))


FFN_CHUNK = 1024


def _ffn_kernel(x_ref, w1_ref, w2_ref, g_ref, b_ref, o_ref, *, alpha):
    x = x_ref[...]
    xb = x.astype(BF16)
    acc = alpha * x
    for c in range(D_FF // FFN_CHUNK):
        cols = slice(c * FFN_CHUNK, (c + 1) * FFN_CHUNK)
        hdn = jnp.maximum(_dot(xb, w1_ref[:, cols]), 0.0)
        acc = acc + _dot((hdn * hdn).astype(BF16), w2_ref[cols, :])
    o_ref[...] = _layer_norm(acc, g_ref[...], b_ref[...])


def _ffn(x2d, w1, w2, g, b, alpha):
    m, d = x2d.shape
    tm = min(ROW_TILE, m)
    row = pl.BlockSpec((tm, d), lambda i: (i, 0))
    return pl.pallas_call(
        functools.partial(_ffn_kernel, alpha=alpha), grid=(m // tm,),
        in_specs=[row, _resident(w1.shape), _resident(w2.shape), _resident((1, d)), _resident((1, d))],
        out_specs=row, out_shape=jax.ShapeDtypeStruct((m, d), F32),
        compiler_params=_params("parallel"), name="ffn")(x2d, w1, w2, g.reshape(1, d), b.reshape(1, d))


def _round_up(n, m):
    return -(-n // m) * m


def _encoder_layer(x3, past, mem_k, mem_v, lw, rope_tabs, alpha):
    b, t, d = x3.shape
    m = b * t
    x2d = x3.reshape(m, d)
    p_len = 0 if past is None else past[0].shape[1]
    l_len = p_len + t
    l_pad = _round_up(l_len, SCAN_BLOCK)
    cos, sin = rope_tabs

    p_mla, p_gdn, p_small, p_fq, p_fk, p_fv = _in_proj(x2d, lw["w_in"])

    q_a, c_kv, kpe128 = _mla_prep(p_mla, cos, sin, lw["qa_g"], lw["kva_g"], lw["w_uq"], t)
    c_kv3 = c_kv.reshape(b, t, MLA_KV_LORA)
    kpe3 = kpe128.reshape(b, t, LANES)
    if past is None:
        ckv_all, kpe_all = c_kv3, kpe3
    else:
        kpe_past = jnp.pad(past[1], ((0, 0), (0, 0), (ROPE_LO, LANES - ROPE_HI)))
        ckv_all = jnp.concatenate([past[0], c_kv3], 1)
        kpe_all = jnp.concatenate([kpe_past, kpe3], 1)
    if l_pad > l_len:
        ckv_all = jnp.pad(ckv_all, ((0, 0), (0, l_pad - l_len), (0, 0)))
        kpe_all = jnp.pad(kpe_all, ((0, 0), (0, l_pad - l_len), (0, 0)))
    k_a, v_a = _mla_kv(ckv_all.reshape(b * l_pad, MLA_KV_LORA), kpe_all.reshape(b * l_pad, LANES), lw["w_ukv"])
    o_a = _flash(q_a.reshape(b, t, -1), k_a.reshape(b, l_pad, -1), v_a.reshape(b, l_pad, -1),
                 causal="chunk", scale=(MLA_NOPE + MLA_ROPE) ** -0.5, packed_qk=False, kv_len=l_len)

    small3 = p_small.reshape(b, t, LANES)
    p_gdn3 = p_gdn.reshape(b, t, IN_GDN_W)
    if past is None:
        conv_past = jnp.zeros((b, CONV_W - 1, GDN_CONV_DIM), F32)
        s_past = jnp.zeros((b, GDN_HEADS, GDN_DK, GDN_DV), F32)
    else:
        s_past, conv_past = past[5], past[6]
    conv0 = jnp.pad(conv_past, ((0, 0), (SUBLANES - (CONV_W - 1), 0), (0, 0)))
    o_b, s_new = _gdn(p_gdn3, small3, lw["conv_w"], lw["gate"], lw["gdn_g"], conv0, s_past, min(CHUNK, t))
    conv_new = jnp.concatenate([conv_past, p_gdn3[:, -(CONV_W - 1):, :GDN_CONV_DIM]], 1)[:, -(CONV_W - 1):]

    if past is None:
        logf_past, fk_all, fv_all = None, p_fk.reshape(b, t, FOX_W), p_fv.reshape(b, t, FOX_W)
    else:
        logf_past = jnp.pad(past[4], ((0, 0), (0, 0), (FF_LANE, LANES - FF_LANE - FOX_HEADS)))
        fk_all = jnp.concatenate([past[2].reshape(b, p_len, FOX_W), p_fk.reshape(b, t, FOX_W)], 1)
        fv_all = jnp.concatenate([past[3].reshape(b, p_len, FOX_W), p_fv.reshape(b, t, FOX_W)], 1)
    if l_pad > l_len:
        fk_all = jnp.pad(fk_all, ((0, 0), (0, l_pad - l_len), (0, 0)))
        fv_all = jnp.pad(fv_all, ((0, 0), (0, l_pad - l_len), (0, 0)))
    logf128, fcum, fcum_t = _fox_scan(small3, lw["fox_bias"], logf_past, l_pad)
    o_c = _flash(p_fq.reshape(b, t, FOX_W), fk_all, fv_all, causal="frame", scale=FOX_HD ** -0.5,
                 packed_qk=True, kv_len=l_len, fq=fcum, fk=fcum_t)

    x1, qm = _out_ln(o_a.reshape(m, MLA_W), o_b.reshape(m, GDN_V), o_c.reshape(m, FOX_W), x2d, lw["w_out"],
                     lw["ln_g"][0], lw["ln_b"][0], lw["w_xq"], alpha)
    x2 = _mem_attn(qm.reshape(b, t, MEM_W), mem_k, mem_v, x1.reshape(b, t, d), lw["w_xo"],
                   lw["ln_g"][1], lw["ln_b"][1], alpha)
    x3_new = _ffn(x2.reshape(m, d), lw["w_ff1"], lw["w_ff2"], lw["ln_g"][2], lw["ln_b"][2], alpha)

    entries = (c_kv3, kpe3[..., ROPE_LO:ROPE_HI], p_fk.reshape(b, t, FOX_HEADS, FOX_HD),
               p_fv.reshape(b, t, FOX_HEADS, FOX_HD), logf128[..., FF_LANE:FF_LANE + FOX_HEADS], s_new, conv_new)
    return x3_new.reshape(b, t, d), entries


def _lane_block(vals, lane):
    return jnp.pad(vals.astype(F32), ((0, 0), (lane, LANES - lane - vals.shape[-1])))


def kernel(x_prompt, x_sample, cache_mla_ckv, cache_mla_kpe, cache_fox_k, cache_fox_v, cache_fox_logf, state_gdn, state_gdn_conv, cache_mem_k, cache_mem_v, mem_prompt, ln_in_g, ln_in_b, w_in, qa_g, kva_g, w_uq, w_ukv, gdn_conv_w, gdn_a_log, gdn_dt_bias, gdn_norm_g, fox_bf, w_out, w_xq, w_mk, w_mv, w_xo, w_ff1, w_ff2, ln_g, ln_b):
    depth = w_in.shape[0]
    alpha = (2 * depth) ** 0.25
    bp, tp, d = x_prompt.shape
    bs, ts, _ = x_sample.shape
    p_len = cache_mla_ckv.shape[2]

    w_in_p = _pad_w_in(w_in)
    w_uq_p = _pad_w_uq(w_uq)
    w_ukv_p = _pad_w_ukv(w_ukv)
    w_mkv = jnp.concatenate([w_mk, w_mv], -1).astype(BF16)
    gate = jnp.stack([_lane_block(gdn_a_log, GA_LANE), _lane_block(gdn_dt_bias, GA_LANE)], 1)
    fox_bias = _lane_block(fox_bf, FF_LANE)[:, None, :]
    bf = lambda a: a.astype(BF16)
    w_out_b, w_xq_b, w_xo_b, w_ff1_b, w_ff2_b = bf(w_out), bf(w_xq), bf(w_xo), bf(w_ff1), bf(w_ff2)

    tabs_p = _rope_tables(tp, 0)
    tabs_s = _rope_tables(ts, p_len)

    xp = _ln_in(x_prompt.reshape(bp * tp, d), ln_in_g, ln_in_b).reshape(bp, tp, d)
    xs = _ln_in(x_sample.reshape(bs * ts, d), ln_in_g, ln_in_b).reshape(bs, ts, d)
    mem2d = mem_prompt.reshape(bp * N_MEM, d)

    p_new, p_mem, s_new = [], [], []
    for l in range(depth):
        lw = dict(w_in=w_in_p[l], qa_g=qa_g[l], kva_g=kva_g[l], w_uq=w_uq_p[l], w_ukv=w_ukv_p[l],
                  conv_w=gdn_conv_w[l], gate=gate[l], gdn_g=gdn_norm_g[l], fox_bias=fox_bias[l],
                  w_out=w_out_b[l], w_xq=w_xq_b[l], w_xo=w_xo_b[l], w_ff1=w_ff1_b[l], w_ff2=w_ff2_b[l],
                  ln_g=ln_g[l], ln_b=ln_b[l])
        mk, mv = _mem_kv(mem2d, w_mkv[l])
        mk3 = mk.reshape(bp, N_MEM, MEM_W)
        mv3 = mv.reshape(bp, N_MEM, MEM_W)
        xp, ent_p = _encoder_layer(xp, None, mk3, mv3, lw, tabs_p, alpha)
        p_new.append(ent_p)
        p_mem.append((mk.reshape(bp, N_MEM, MEM_HEADS, MEM_HD), mv.reshape(bp, N_MEM, MEM_HEADS, MEM_HD)))
        past = (cache_mla_ckv[l], cache_mla_kpe[l], cache_fox_k[l], cache_fox_v[l], cache_fox_logf[l],
                state_gdn[l], state_gdn_conv[l])
        xs, ent_s = _encoder_layer(xs, past, cache_mem_k[l].reshape(bs, N_MEM, MEM_W),
                                   cache_mem_v[l].reshape(bs, N_MEM, MEM_W), lw, tabs_s, alpha)
        s_new.append(ent_s)

    stack = lambda entries, i: jnp.stack([e[i] for e in entries])
    return (xp, xs,
            stack(p_new, 0), stack(p_new, 1), stack(p_new, 2), stack(p_new, 3), stack(p_new, 4),
            stack(p_new, 5), stack(p_new, 6), stack(p_mem, 0), stack(p_mem, 1),
            stack(s_new, 0), stack(s_new, 1), stack(s_new, 2), stack(s_new, 3), stack(s_new, 4),
            stack(s_new, 5), stack(s_new, 6))
```

```python
import functools
import math

import jax
import jax.numpy as jnp
from jax import lax
from jax.experimental import pallas as pl
from jax.experimental.pallas import tpu as pltpu

F32 = jnp.float32
BF16 = jnp.bfloat16

D_MODEL = 1024
CHUNK = 64
MLA_HEADS = 4
MLA_NOPE = 64
MLA_ROPE = 32
MLA_V = 64
MLA_Q_LORA = 384
MLA_KV_LORA = 256
ROPE_THETA = 10000.0
GDN_HEADS = 4
GDN_DK = 128
GDN_DV = 128
CONV_W = 4
FOX_HEADS = 4
FOX_HD = 64
N_MEM = 256
MEM_HEADS = 4
MEM_HD = 128
D_FF = 4 * D_MODEL
LN_EPS = 1e-5
RMS_EPS = 1e-6
L2_EPS = 1e-6

GDN_QK = GDN_HEADS * GDN_DK
GDN_V = GDN_HEADS * GDN_DV
GDN_CONV_DIM = 2 * GDN_QK + GDN_V
FOX_W = FOX_HEADS * FOX_HD
MLA_W = MLA_HEADS * MLA_V
MEM_W = MEM_HEADS * MEM_HD

LANES = 128
SUBLANES = 8
ROW_TILE = 512
ATTN_BLOCK = 512
FLASH_ROW_SPLIT = 2
SCAN_BLOCK = 256
VMEM_LIMIT = 56 * 1024 * 1024
NEG_BIG = -1e30
LOG2E = 1.0 / math.log(2.0)

FF_LANE = 0
GA_LANE = 4
GB_LANE = 8
ROPE_LO = MLA_NOPE
ROPE_HI = MLA_NOPE + MLA_ROPE


def _params(*sem):
    return pltpu.CompilerParams(dimension_semantics=sem, vmem_limit_bytes=VMEM_LIMIT)


def _resident(shape):
    nd = len(shape)
    return pl.BlockSpec(shape, lambda *_: (0,) * nd)


def _layer_norm(y, g, b):
    mu = jnp.mean(y, axis=-1, keepdims=True)
    yc = y - mu
    var = jnp.mean(yc * yc, axis=-1, keepdims=True)
    return yc * lax.rsqrt(var + LN_EPS) * g + b


def _sigmoid(x):
    return 1.0 / (1.0 + jnp.exp(-x))


def _softplus(x):
    return jnp.maximum(x, 0.0) + jnp.log1p(jnp.exp(-jnp.abs(x)))


def _dot(a, b):
    return jnp.dot(a, b, preferred_element_type=F32)


def _dot_nt(a, b):
    return lax.dot_general(a, b, (((1,), (1,)), ((), ())), preferred_element_type=F32)


def _split3(x):
    hi = x.astype(BF16)
    r1 = x - hi.astype(F32)
    mid = r1.astype(BF16)
    lo = (r1 - mid.astype(F32)).astype(BF16)
    return hi, mid, lo


def _cumsum_rows(tri, x):
    hi, mid, lo = _split3(x)
    return _dot(tri, hi) + _dot(tri, mid) + _dot(tri, lo)


def _tri(n):
    r = lax.broadcasted_iota(jnp.int32, (n, n), 0)
    c = lax.broadcasted_iota(jnp.int32, (n, n), 1)
    return (r >= c).astype(BF16)


def _rope_table_kernel(cos_ref, sin_ref, *, start):
    shape = cos_ref.shape
    lane = lax.broadcasted_iota(jnp.int32, shape, 1)
    row = lax.broadcasted_iota(jnp.int32, shape, 0)
    half = MLA_ROPE // 2
    idx = ((lane - ROPE_LO) & (half - 1)).astype(F32)
    inv = jnp.exp(idx * (-(2.0 / MLA_ROPE) * math.log(ROPE_THETA)))
    ang = (row + start).astype(F32) * inv
    in_rope = (lane >= ROPE_LO) & (lane < ROPE_HI)
    cos_ref[...] = jnp.where(in_rope, jnp.cos(ang), 1.0)
    sin_ref[...] = jnp.where(in_rope, jnp.sin(ang), 0.0)


def _rope_tables(t, start):
    shp = jax.ShapeDtypeStruct((t, LANES), F32)
    return pl.pallas_call(functools.partial(_rope_table_kernel, start=start),
                          out_shape=(shp, shp), name="rope_tables")()


def _ln_kernel(x_ref, g_ref, b_ref, o_ref):
    o_ref[...] = _layer_norm(x_ref[...], g_ref[...], b_ref[...])


def _ln_in(x2d, g, b):
    m, d = x2d.shape
    tm = min(ROW_TILE, m)
    return pl.pallas_call(
        _ln_kernel, grid=(m // tm,),
        in_specs=[pl.BlockSpec((tm, d), lambda i: (i, 0)), _resident((1, d)), _resident((1, d))],
        out_specs=pl.BlockSpec((tm, d), lambda i: (i, 0)),
        out_shape=jax.ShapeDtypeStruct((m, d), F32),
        compiler_params=_params("parallel"), name="ln_in")(x2d, g.reshape(1, d), b.reshape(1, d))


IN_MLA_W = MLA_Q_LORA + MLA_KV_LORA + 2 * LANES
IN_GDN_W = GDN_CONV_DIM + GDN_V
IN_PAD_COLS = (IN_MLA_W, IN_GDN_W, LANES, FOX_W, FOX_W, FOX_W)
IN_PAD_OFFS = tuple(sum(IN_PAD_COLS[:n]) for n in range(len(IN_PAD_COLS) + 1))


def _in_proj_kernel(x_ref, w_ref, *o_refs):
    xb = x_ref[...].astype(BF16)
    for o_ref, lo, hi in zip(o_refs, IN_PAD_OFFS[:-1], IN_PAD_OFFS[1:]):
        o_ref[...] = _dot(xb, w_ref[:, lo:hi])


def _in_proj(x2d, w_pad):
    m, d = x2d.shape
    tm = min(ROW_TILE, m)
    return pl.pallas_call(
        _in_proj_kernel, grid=(m // tm,),
        in_specs=[pl.BlockSpec((tm, d), lambda i: (i, 0)), _resident(w_pad.shape)],
        out_specs=[pl.BlockSpec((tm, w), lambda i: (i, 0)) for w in IN_PAD_COLS],
        out_shape=[jax.ShapeDtypeStruct((m, w), F32) for w in IN_PAD_COLS],
        compiler_params=_params("parallel"), name="in_proj")(x2d, w_pad)


def _pad_w_in(w_in):
    offs = [0]
    for s in (MLA_Q_LORA, MLA_KV_LORA, MLA_ROPE, GDN_QK, GDN_QK, GDN_V, GDN_V, GDN_HEADS, GDN_HEADS,
              FOX_W, FOX_W, FOX_W, FOX_HEADS):
        offs.append(offs[-1] + s)
    cq, ckv, kpe, gq, gk, gv, gz, ga, gb, fq, fk, fv, ff = [w_in[..., a:b] for a, b in zip(offs[:-1], offs[1:])]
    half = MLA_ROPE // 2
    kpe_rot = jnp.concatenate([-kpe[..., half:], kpe[..., :half]], -1)

    def z(n):
        return jnp.zeros(w_in.shape[:-1] + (n,), w_in.dtype)

    def rope_block(w):
        return jnp.concatenate([z(ROPE_LO), w, z(LANES - ROPE_HI)], -1)

    small = jnp.concatenate([ff, ga, gb, z(LANES - 3 * GDN_HEADS)], -1)
    return jnp.concatenate([cq, ckv, rope_block(kpe), rope_block(kpe_rot), gq, gk, gv, gz, small, fq, fk, fv],
                           -1).astype(BF16)


def _rms(x, g):
    return x * lax.rsqrt(jnp.mean(x * x, axis=-1, keepdims=True) + RMS_EPS) * g


def _mla_prep_kernel(p_ref, cos_ref, sin_ref, qg_ref, kg_ref, wq_ref, q_ref, ckv_ref, kpe_ref):
    cos = cos_ref[...]
    sin = sin_ref[...]
    c_q = _rms(p_ref[:, :MLA_Q_LORA], qg_ref[...]).astype(BF16)
    qq = _dot(c_q, wq_ref[...])
    hw = MLA_HEADS * LANES
    for h in range(MLA_HEADS):
        a = qq[:, h * LANES:(h + 1) * LANES]
        b = qq[:, hw + h * LANES:hw + (h + 1) * LANES]
        q_ref[:, h * LANES:(h + 1) * LANES] = (a * cos + b * sin).astype(BF16)
    lo = MLA_Q_LORA
    ckv_ref[...] = _rms(p_ref[:, lo:lo + MLA_KV_LORA], kg_ref[...])
    lo += MLA_KV_LORA
    kpe_ref[...] = p_ref[:, lo:lo + LANES] * cos + p_ref[:, lo + LANES:lo + 2 * LANES] * sin


def _mla_prep(p_mla, cos, sin, qa_g, kva_g, w_uq2, t):
    m = p_mla.shape[0]
    tm = min(ROW_TILE, t)
    nt = t // tm
    row = lambda w: pl.BlockSpec((tm, w), lambda i: (i, 0))
    tab = pl.BlockSpec((tm, LANES), lambda i: (i % nt, 0))
    return pl.pallas_call(
        _mla_prep_kernel, grid=(m // tm,),
        in_specs=[row(IN_MLA_W), tab, tab, _resident((1, MLA_Q_LORA)), _resident((1, MLA_KV_LORA)),
                  _resident(w_uq2.shape)],
        out_specs=[row(MLA_HEADS * LANES), row(MLA_KV_LORA), row(LANES)],
        out_shape=[jax.ShapeDtypeStruct((m, MLA_HEADS * LANES), BF16),
                   jax.ShapeDtypeStruct((m, MLA_KV_LORA), F32),
                   jax.ShapeDtypeStruct((m, LANES), F32)],
        compiler_params=_params("parallel"), name="mla_prep")(
            p_mla, cos, sin, qa_g.reshape(1, -1), kva_g.reshape(1, -1), w_uq2)


def _pad_w_uq(w_uq):
    half = MLA_ROPE // 2
    w = w_uq.reshape(w_uq.shape[:-1] + (MLA_HEADS, MLA_NOPE + MLA_ROPE))
    nope, rp = w[..., :MLA_NOPE], w[..., MLA_NOPE:]
    rot = jnp.concatenate([-rp[..., half:], rp[..., :half]], -1)
    zpad = jnp.zeros(w.shape[:-1] + (LANES - ROPE_HI,), w.dtype)
    plain = jnp.concatenate([nope, rp, zpad], -1)
    rotated = jnp.concatenate([jnp.zeros_like(nope), rot, zpad], -1)
    flat = lambda a: a.reshape(a.shape[:-2] + (MLA_HEADS * LANES,))
    return jnp.concatenate([flat(plain), flat(rotated)], -1).astype(BF16)


def _mla_kv_kernel(ckv_ref, kpe_ref, w_ref, k_ref, v_ref):
    kv = _dot(ckv_ref[...].astype(BF16), w_ref[...])
    kpe = kpe_ref[...]
    kw = MLA_HEADS * LANES
    for h in range(MLA_HEADS):
        k_ref[:, h * LANES:(h + 1) * LANES] = (kv[:, h * LANES:(h + 1) * LANES] + kpe).astype(BF16)
    v_ref[...] = kv[:, kw:].astype(BF16)


def _mla_kv(ckv_all, kpe_all, w_ukv2):
    r = ckv_all.shape[0]
    tm = min(ROW_TILE, r)
    row = lambda w: pl.BlockSpec((tm, w), lambda i: (i, 0))
    return pl.pallas_call(
        _mla_kv_kernel, grid=(r // tm,),
        in_specs=[row(MLA_KV_LORA), row(LANES), _resident(w_ukv2.shape)],
        out_specs=[row(MLA_HEADS * LANES), row(MLA_W)],
        out_shape=[jax.ShapeDtypeStruct((r, MLA_HEADS * LANES), BF16), jax.ShapeDtypeStruct((r, MLA_W), BF16)],
        compiler_params=_params("parallel"), name="mla_kv")(ckv_all, kpe_all, w_ukv2)


def _pad_w_ukv(w_ukv):
    w = w_ukv.reshape(w_ukv.shape[:-1] + (MLA_HEADS, MLA_NOPE + MLA_V))
    nope, v = w[..., :MLA_NOPE], w[..., MLA_NOPE:]
    kpad = jnp.concatenate([nope, jnp.zeros(nope.shape[:-1] + (LANES - MLA_NOPE,), w.dtype)], -1)
    flat = lambda a: a.reshape(a.shape[:-2] + (-1,))
    return jnp.concatenate([flat(kpad), flat(v)], -1).astype(BF16)


def _flash_kernel(*refs, tq, tk, q_start, kv_len, kv_pad, causal, scale, packed_qk, has_f, heads):
    if has_f:
        q_ref, k_ref, v_ref, fq_ref, fk_ref, o_ref, m_sc, acc_sc, fq_sc = refs
    else:
        q_ref, k_ref, v_ref, o_ref, m_sc, acc_sc = refs
    i = pl.program_id(1)
    j = pl.program_id(2)
    nk = pl.num_programs(2)

    @pl.when(j == 0)
    def _():
        m_sc[...] = jnp.full(m_sc.shape, NEG_BIG, F32)
        acc_sc[...] = jnp.zeros(acc_sc.shape, F32)
        if has_f:
            for h in range(heads):
                fq_sc[h] = jnp.broadcast_to(fq_ref[0, :, FF_LANE + h:FF_LANE + h + 1], (tq, LANES))

    qmin = q_start + i * tq
    qmax = qmin + tq - 1
    kmin = j * tk
    kmax = kmin + tk - 1
    if causal == "frame":
        any_vis = kmin <= qmax
        all_vis = kmax <= qmin
    else:
        any_vis = (kmin // CHUNK) <= (qmax // CHUNK)
        all_vis = (kmax // CHUNK) <= (qmin // CHUNK)
    if kv_len < kv_pad:
        all_vis = jnp.logical_and(all_vis, kmax < kv_len)

    fold = math.log2(scale) == round(math.log2(scale))
    half_lane = lax.broadcasted_iota(jnp.int32, (1, LANES), 1) < (LANES // 2)

    def body(masked):
        if masked:
            qpos = qmin + lax.broadcasted_iota(jnp.int32, (tq, tk), 0)
            kpos = kmin + lax.broadcasted_iota(jnp.int32, (tq, tk), 1)
            if causal == "frame":
                vis = kpos <= qpos
            else:
                vis = (kpos // CHUNK) <= (qpos // CHUNK)
            if kv_len < kv_pad:
                vis = jnp.logical_and(vis, kpos < kv_len)
        nsplit = FLASH_ROW_SPLIT if tq % (FLASH_ROW_SPLIT * LANES) == 0 else 1
        rs = tq // nsplit
        units = [(h, r) for h in range(heads) for r in range(nsplit)]
        c = LOG2E if fold else LOG2E * scale
        k_cache, v_cache = {}, {}

        def k_block(b0):
            if b0 not in k_cache:
                k_cache[b0] = k_ref[0, :, b0 * LANES:(b0 + 1) * LANES].astype(BF16)
            return k_cache[b0]

        def v_block(h):
            if h not in v_cache:
                vpair = v_ref[0, :, (h // 2) * LANES:(h // 2 + 1) * LANES].astype(BF16)
                mine = half_lane if h % 2 == 0 else jnp.logical_not(half_lane)
                v_cache[h] = jnp.where(mine, vpair, jnp.ones_like(vpair))
            return v_cache[h]

        def scores(h, r):
            rows = slice(r * rs, (r + 1) * rs)
            b0 = h // 2 if packed_qk else h
            qh = q_ref[0, rows, b0 * LANES:(b0 + 1) * LANES]
            if packed_qk:
                qh = jnp.where(half_lane if h % 2 == 0 else jnp.logical_not(half_lane), qh, 0.0)
            if fold:
                qh = qh * scale
            return _dot_nt(qh.astype(BF16), k_block(b0))

        def softmax_part(h, r, s):
            rows = slice(r * rs, (r + 1) * rs)
            if has_f:
                s = s - fk_ref[0, FF_LANE + h:FF_LANE + h + 1, :]
            if masked:
                s = jnp.where(vis[rows], s, NEG_BIG)
            m_old = m_sc[h, rows]
            row_max = jnp.max(s, axis=-1, keepdims=True)
            if has_f:
                fq_b = fq_sc[h, rows]
                m_new = jnp.maximum(m_old, row_max + fq_b)
                shift = m_new - fq_b
            else:
                m_new = jnp.maximum(m_old, row_max)
                shift = m_new
            m_sc[h, rows] = m_new
            alpha = jnp.exp2((m_old - m_new) * c)
            e = jnp.concatenate([jnp.exp2((s[:, b * LANES:(b + 1) * LANES] - shift) * c)
                                 for b in range(tk // LANES)], -1)
            return alpha, e.astype(BF16)

        def weighted_values(h, r, alpha, e):
            rows = slice(r * rs, (r + 1) * rs)
            acc_sc[h, rows] = acc_sc[h, rows] * alpha + _dot(e, v_block(h))

        s_q, e_q = {}, {}
        for n in range(len(units) + 2):
            if n < len(units):
                s_q[n] = scores(*units[n])
            if 0 <= n - 1 < len(units):
                e_q[n - 1] = softmax_part(*units[n - 1], s_q.pop(n - 1))
            if 0 <= n - 2 < len(units):
                weighted_values(*units[n - 2], *e_q.pop(n - 2))

    @pl.when(all_vis)
    def _():
        body(False)

    @pl.when(jnp.logical_and(any_vis, jnp.logical_not(all_vis)))
    def _():
        body(True)

    @pl.when(j == nk - 1)
    def _():
        for pair in range(heads // 2):
            a_lo = acc_sc[2 * pair]
            a_hi = acc_sc[2 * pair + 1]
            lo = a_lo / pltpu.roll(a_lo, LANES // 2, 1)
            hi = a_hi / pltpu.roll(a_hi, LANES // 2, 1)
            o_ref[0, :, pair * LANES:(pair + 1) * LANES] = jnp.where(half_lane, lo, hi).astype(o_ref.dtype)


def _flash(q, k, v, *, causal, scale, packed_qk, kv_len, fq=None, fk=None, tq=None, tk=None):
    b, t, wq = q.shape
    lp = k.shape[1]
    heads = v.shape[2] // (LANES // 2)
    tq = tq or min(ATTN_BLOCK, t)
    tk = tk or (min(ATTN_BLOCK, lp) if t >= ATTN_BLOCK else lp)
    q_start = kv_len - t
    nq, nk = t // tq, lp // tk

    def last_block(i):
        qmax = q_start + i * tq + tq - 1
        last = qmax if causal == "frame" else (qmax // CHUNK) * CHUNK + CHUNK - 1
        return jnp.minimum(last // tk, nk - 1)

    kv_map = lambda bi, i, j: (bi, jnp.minimum(j, last_block(i)), 0)
    in_specs = [pl.BlockSpec((1, tq, wq), lambda bi, i, j: (bi, i, 0)),
                pl.BlockSpec((1, tk, wq), kv_map),
                pl.BlockSpec((1, tk, v.shape[2]), kv_map)]
    args = [q, k, v]
    has_f = fq is not None
    if has_f:
        in_specs += [pl.BlockSpec((1, tq, LANES), lambda bi, i, j: (bi, i + q_start // tq, 0)),
                     pl.BlockSpec((1, SUBLANES, tk), lambda bi, i, j: (bi, 0, jnp.minimum(j, last_block(i))))]
        args += [fq, fk]
        assert q_start % tq == 0
    kern = functools.partial(_flash_kernel, tq=tq, tk=tk, q_start=q_start, kv_len=kv_len, kv_pad=lp,
                             causal=causal, scale=scale, packed_qk=packed_qk, has_f=has_f, heads=heads)
    return pl.pallas_call(
        kern, grid=(b, nq, nk), in_specs=in_specs,
        out_specs=pl.BlockSpec((1, tq, v.shape[2]), lambda bi, i, j: (bi, i, 0)),
        out_shape=jax.ShapeDtypeStruct((b, t, v.shape[2]), BF16),
        scratch_shapes=[pltpu.VMEM((heads, tq, LANES), F32)] * (3 if has_f else 2),
        compiler_params=_params("parallel", "parallel", "arbitrary"),
        name="flash_" + causal)(*args)


def _fox_scan_kernel(*refs, p_len, t_len, l_pad):
    if p_len:
        small_ref, bias_ref, past_ref, logf_ref, fc_ref, fct_ref = refs
    else:
        small_ref, bias_ref, logf_ref, fc_ref, fct_ref = refs
    x = small_ref[0] + bias_ref[...]
    logf = jnp.minimum(x, 0.0) - jnp.log1p(jnp.exp(-jnp.abs(x)))
    logf_ref[0] = logf
    bs = SCAN_BLOCK
    tri = _tri(bs)
    carry = jnp.zeros((1, LANES), F32)
    for blk in range(l_pad // bs):
        lo = blk * bs
        if lo + bs <= p_len:
            xb = past_ref[0, lo:lo + bs, :]
        elif lo >= p_len and lo + bs <= p_len + t_len:
            xb = logf[lo - p_len:lo - p_len + bs]
        else:
            assert lo == p_len and t_len < bs
            xb = jnp.concatenate([logf, jnp.zeros((bs - t_len, LANES), F32)], 0)
        cs = _cumsum_rows(tri, xb) + carry
        carry = cs[bs - 1:bs, :]
        fc_ref[0, lo:lo + bs, :] = cs
        fct_ref[0, :, lo:lo + bs] = cs.T[:SUBLANES, :]


def _fox_scan(small3, bias128, past128, l_pad):
    b, t, _ = small3.shape
    p_len = 0 if past128 is None else past128.shape[1]
    per_b = lambda r, w: pl.BlockSpec((1, r, w), lambda bi: (bi, 0, 0))
    in_specs = [per_b(t, LANES), _resident((1, LANES))]
    args = [small3, bias128]
    if p_len:
        in_specs.append(per_b(p_len, LANES))
        args.append(past128)
    return pl.pallas_call(
        functools.partial(_fox_scan_kernel, p_len=p_len, t_len=t, l_pad=l_pad), grid=(b,),
        in_specs=in_specs,
        out_specs=[per_b(t, LANES), per_b(l_pad, LANES), per_b(SUBLANES, l_pad)],
        out_shape=[jax.ShapeDtypeStruct((b, t, LANES), F32), jax.ShapeDtypeStruct((b, l_pad, LANES), F32),
                   jax.ShapeDtypeStruct((b, SUBLANES, l_pad), F32)],
        compiler_params=_params("parallel"), name="fox_scan")(*args)


GDN_BLOCK = 256


def _gdn_kernel(qkv_ref, z_ref, small_ref, convw_ref, gate_ref, ng_ref, conv0_ref, s0_ref,
                o_ref, sout_ref, prev_sc, s_sc, *, blk, c_len):
    step = pl.program_id(1)
    nsteps = pl.num_programs(1)
    nsub = blk // c_len

    @pl.when(step == 0)
    def _():
        prev_sc[...] = jnp.zeros(prev_sc.shape, F32)
        prev_sc[blk - SUBLANES:, :] = conv0_ref[0]
        s_sc[...] = s0_ref[0]

    x = qkv_ref[0]
    prev = prev_sc[...]
    row = lax.broadcasted_iota(jnp.int32, (blk, 1), 0)
    y = x * convw_ref[CONV_W - 1:CONV_W, :]
    for s in range(1, CONV_W):
        shifted = jnp.where(row < s, pltpu.roll(prev, s, 0), pltpu.roll(x, s, 0))
        y = y + shifted * convw_ref[CONV_W - 1 - s:CONV_W - s, :]
    prev_sc[...] = x
    y = y * _sigmoid(y)

    ri = lax.broadcasted_iota(jnp.int32, (blk, blk), 0)
    ci = lax.broadcasted_iota(jnp.int32, (blk, blk), 1)
    same = (ri // c_len) == (ci // c_len)
    incl = jnp.logical_and(ri >= ci, same)
    strict = jnp.logical_and(ri > ci, same)

    small = small_ref[0]
    g128 = -jnp.exp(gate_ref[0:1, :]) * _softplus(small + gate_ref[1:2, :])
    beta128 = _sigmoid(small)
    gcum = _cumsum_rows(incl.astype(BF16), g128)
    gcum_t = gcum.T
    eg128 = jnp.exp(gcum)
    glast128 = jnp.concatenate(
        [jnp.broadcast_to(gcum[(c + 1) * c_len - 1:(c + 1) * c_len, :], (c_len, LANES)) for c in range(nsub)], 0)
    ekd128 = jnp.exp(glast128 - gcum)
    nlev = int(math.log2(c_len))
    assert 2 ** nlev == c_len

    heads = range(GDN_HEADS)
    kh_l, kb_l, a_l, decay_l, rhs_l, qcb_l, qd_l, kdt_l = [], [], [], [], [], [], [], []
    for h in heads:
        qh = y[:, h * GDN_DK:(h + 1) * GDN_DK]
        kh = y[:, GDN_QK + h * GDN_DK:GDN_QK + (h + 1) * GDN_DK]
        vh = y[:, 2 * GDN_QK + h * GDN_DV:2 * GDN_QK + (h + 1) * GDN_DV]
        qh = qh * lax.rsqrt(jnp.sum(qh * qh, axis=-1, keepdims=True) + L2_EPS)
        kh = kh * lax.rsqrt(jnp.sum(kh * kh, axis=-1, keepdims=True) + L2_EPS)
        lane = GA_LANE + h
        g_col = gcum[:, lane:lane + 1]
        g_row = gcum_t[lane:lane + 1, :]
        eg_col = eg128[:, lane:lane + 1]
        beta = beta128[:, GB_LANE + h:GB_LANE + h + 1]
        decay = jnp.exp(jnp.where(incl, g_col - g_row, NEG_BIG))
        kb = kh.astype(BF16)
        a_l.append(jnp.where(strict, beta * _dot_nt(kb, kb) * decay, 0.0))
        rhs_l.append(jnp.concatenate([vh * beta, kh * (beta * eg_col)], -1))
        qc = qh * (GDN_DK ** -0.5)
        qcb_l.append(qc.astype(BF16))
        qd_l.append((qc * eg_col).astype(BF16))
        kdt_l.append((kh * ekd128[:, lane:lane + 1]).T.astype(BF16))
        kb_l.append(kb)
        decay_l.append(decay)

    tm_l = [-a for a in a_l]
    pw_l = a_l
    for _ in range(1, nlev):
        pb_l = [pw.astype(BF16) for pw in pw_l]
        pw_l = [_dot(pb, pb) for pb in pb_l]
        tm_l = [tm + pw + _dot(tm.astype(BF16), pw.astype(BF16)) for tm, pw in zip(tm_l, pw_l)]
    sol_l = [rhs + _dot(tm.astype(BF16), rhs.astype(BF16)) for tm, rhs in zip(tm_l, rhs_l)]
    u_l = [sol[:, :GDN_DV] for sol in sol_l]
    wb_l = [sol[:, GDN_DV:].astype(BF16) for sol in sol_l]
    qk_l = [(_dot_nt(qcb, kb) * decay).astype(BF16) for qcb, kb, decay in zip(qcb_l, kb_l, decay_l)]

    s_l = [s_sc[h] for h in heads]
    v_parts = [[] for _ in heads]
    qs_parts = [[] for _ in heads]
    for c in range(nsub):
        rows = slice(c * c_len, (c + 1) * c_len)
        r_l = [_dot(jnp.concatenate([wb_l[h][rows], qd_l[h][rows]], 0), s_l[h].astype(BF16)) for h in heads]
        for h in heads:
            v_parts[h].append(u_l[h][rows] - r_l[h][:c_len])
            qs_parts[h].append(r_l[h][c_len:])
        s_l = [s_l[h] * eg128[(c + 1) * c_len - 1:(c + 1) * c_len, GA_LANE + h:GA_LANE + h + 1]
               + _dot(kdt_l[h][:, rows], v_parts[h][c].astype(BF16)) for h in heads]
    for h in heads:
        s_sc[h] = s_l[h]
        v_all = v_parts[h][0] if nsub == 1 else jnp.concatenate(v_parts[h], 0)
        qs_all = qs_parts[h][0] if nsub == 1 else jnp.concatenate(qs_parts[h], 0)
        o = qs_all + _dot(qk_l[h], v_all.astype(BF16))
        zh = z_ref[0, :, h * GDN_DV:(h + 1) * GDN_DV]
        o = _rms(o, ng_ref[...]) * (zh * _sigmoid(zh))
        o_ref[0, :, h * GDN_DV:(h + 1) * GDN_DV] = o.astype(o_ref.dtype)

    @pl.when(step == nsteps - 1)
    def _():
        sout_ref[0] = s_sc[...]


def _gdn(p_gdn3, small3, conv_w, gate2, norm_g, conv0, s0, c_len):
    b, t, _ = p_gdn3.shape
    blk = min(GDN_BLOCK, t)
    return pl.pallas_call(
        functools.partial(_gdn_kernel, blk=blk, c_len=c_len), grid=(b, t // blk),
        in_specs=[pl.BlockSpec((1, blk, GDN_CONV_DIM), lambda bi, c: (bi, c, 0)),
                  pl.BlockSpec((1, blk, GDN_V), lambda bi, c: (bi, c, GDN_CONV_DIM // GDN_V)),
                  pl.BlockSpec((1, blk, LANES), lambda bi, c: (bi, c, 0)),
                  _resident((CONV_W, GDN_CONV_DIM)), _resident((2, LANES)), _resident((1, GDN_DV)),
                  pl.BlockSpec((1, SUBLANES, GDN_CONV_DIM), lambda bi, c: (bi, 0, 0)),
                  pl.BlockSpec((1, GDN_HEADS, GDN_DK, GDN_DV), lambda bi, c: (bi, 0, 0, 0))],
        out_specs=[pl.BlockSpec((1, blk, GDN_V), lambda bi, c: (bi, c, 0)),
                   pl.BlockSpec((1, GDN_HEADS, GDN_DK, GDN_DV), lambda bi, c: (bi, 0, 0, 0))],
        out_shape=[jax.ShapeDtypeStruct((b, t, GDN_V), BF16),
                   jax.ShapeDtypeStruct((b, GDN_HEADS, GDN_DK, GDN_DV), F32)],
        scratch_shapes=[pltpu.VMEM((blk, GDN_CONV_DIM), F32), pltpu.VMEM((GDN_HEADS, GDN_DK, GDN_DV), F32)],
        compiler_params=_params("parallel", "arbitrary"), name="gdn")(
            p_gdn3, p_gdn3, small3, conv_w, gate2, norm_g.reshape(1, -1), conv0, s0)


def _out_ln_kernel(oa_ref, ob_ref, oc_ref, x_ref, w_ref, g_ref, b_ref, wq_ref, x1_ref, qm_ref, *, alpha):
    mix = (_dot(oa_ref[...], w_ref[:MLA_W, :]) + _dot(ob_ref[...], w_ref[MLA_W:MLA_W + GDN_V, :])
           + _dot(oc_ref[...], w_ref[MLA_W + GDN_V:, :]))
    x1 = _layer_norm(alpha * x_ref[...] + mix, g_ref[...], b_ref[...])
    x1_ref[...] = x1
    qm_ref[...] = _dot(x1.astype(BF16), wq_ref[...]).astype(BF16)


def _out_ln(o_a, o_b, o_c, x2d, w_out, g, b, w_xq, alpha):
    m, d = x2d.shape
    tm = min(ROW_TILE, m)
    row = lambda w: pl.BlockSpec((tm, w), lambda i: (i, 0))
    return pl.pallas_call(
        functools.partial(_out_ln_kernel, alpha=alpha), grid=(m // tm,),
        in_specs=[row(MLA_W), row(GDN_V), row(FOX_W), row(d), _resident(w_out.shape), _resident((1, d)),
                  _resident((1, d)), _resident(w_xq.shape)],
        out_specs=[row(d), row(MEM_W)],
        out_shape=[jax.ShapeDtypeStruct((m, d), F32), jax.ShapeDtypeStruct((m, MEM_W), BF16)],
        compiler_params=_params("parallel"), name="out_ln")(
            o_a, o_b, o_c, x2d, w_out, g.reshape(1, d), b.reshape(1, d), w_xq)


def _mem_kv_kernel(m_ref, w_ref, k_ref, v_ref):
    kv = _dot(m_ref[...].astype(BF16), w_ref[...])
    k_ref[...] = kv[:, :MEM_W]
    v_ref[...] = kv[:, MEM_W:]


def _mem_kv(mem2d, w_mkv):
    m, d = mem2d.shape
    tm = min(ROW_TILE, m)
    row = lambda w: pl.BlockSpec((tm, w), lambda i: (i, 0))
    shp = jax.ShapeDtypeStruct((m, MEM_W), F32)
    return pl.pallas_call(
        _mem_kv_kernel, grid=(m // tm,),
        in_specs=[row(d), _resident(w_mkv.shape)], out_specs=[row(MEM_W), row(MEM_W)], out_shape=[shp, shp],
        compiler_params=_params("parallel"), name="mem_kv")(mem2d, w_mkv)


def _mem_attn_kernel(q_ref, k_ref, v_ref, x_ref, w_ref, g_ref, b_ref, o_ref, *, alpha):
    scale = MEM_HD ** -0.5
    outs = []
    for h in range(MEM_HEADS):
        cols = slice(h * MEM_HD, (h + 1) * MEM_HD)
        s = _dot_nt(q_ref[0, :, cols], k_ref[0, :, cols].astype(BF16)) * scale
        e = jnp.exp(s - jnp.max(s, axis=-1, keepdims=True))
        p = e / jnp.sum(e, axis=-1, keepdims=True)
        outs.append(_dot(p.astype(BF16), v_ref[0, :, cols].astype(BF16)))
    o = jnp.concatenate(outs, -1).astype(BF16)
    y = alpha * x_ref[0] + _dot(o, w_ref[...])
    o_ref[0] = _layer_norm(y, g_ref[...], b_ref[...])


def _mem_attn(qm3, mk3, mv3, x3, w_xo, g, b, alpha):
    bsz, t, d = x3.shape
    tq = min(ROW_TILE, t)
    nm = mk3.shape[1]
    return pl.pallas_call(
        functools.partial(_mem_attn_kernel, alpha=alpha), grid=(bsz, t // tq),
        in_specs=[pl.BlockSpec((1, tq, MEM_W), lambda bi, i: (bi, i, 0)),
                  pl.BlockSpec((1, nm, MEM_W), lambda bi, i: (bi, 0, 0)),
                  pl.BlockSpec((1, nm, MEM_W), lambda bi, i: (bi, 0, 0)),
                  pl.BlockSpec((1, tq, d), lambda bi, i: (bi, i, 0)),
                  _resident(w_xo.shape), _resident((1, d)), _resident((1, d))],
        out_specs=pl.BlockSpec((1, tq, d), lambda bi, i: (bi, i, 0)),
        out_shape=jax.ShapeDtypeStruct((bsz, t, d), F32),
        compiler_params=_params("parallel", "parallel"), name="mem_attn")(
            qm3, mk3, mv3, x3, w_xo, g.reshape(1, d), b.reshape(1, d))


FFN_CHUNK = 1024


def _ffn_kernel(x_ref, w1_ref, w2_ref, g_ref, b_ref, o_ref, *, alpha):
    x = x_ref[...]
    xb = x.astype(BF16)
    acc = alpha * x
    for c in range(D_FF // FFN_CHUNK):
        cols = slice(c * FFN_CHUNK, (c + 1) * FFN_CHUNK)
        hdn = jnp.maximum(_dot(xb, w1_ref[:, cols]), 0.0)
        acc = acc + _dot((hdn * hdn).astype(BF16), w2_ref[cols, :])
    o_ref[...] = _layer_norm(acc, g_ref[...], b_ref[...])


def _ffn(x2d, w1, w2, g, b, alpha):
    m, d = x2d.shape
    tm = min(ROW_TILE, m)
    row = pl.BlockSpec((tm, d), lambda i: (i, 0))
    return pl.pallas_call(
        functools.partial(_ffn_kernel, alpha=alpha), grid=(m // tm,),
        in_specs=[row, _resident(w1.shape), _resident(w2.shape), _resident((1, d)), _resident((1, d))],
        out_specs=row, out_shape=jax.ShapeDtypeStruct((m, d), F32),
        compiler_params=_params("parallel"), name="ffn")(x2d, w1, w2, g.reshape(1, d), b.reshape(1, d))


def _round_up(n, m):
    return -(-n // m) * m


def _encoder_layer(x3, past, mem_k, mem_v, lw, rope_tabs, alpha):
    b, t, d = x3.shape
    m = b * t
    x2d = x3.reshape(m, d)
    p_len = 0 if past is None else past[0].shape[1]
    l_len = p_len + t
    l_pad = _round_up(l_len, SCAN_BLOCK)
    cos, sin = rope_tabs

    p_mla, p_gdn, p_small, p_fq, p_fk, p_fv = _in_proj(x2d, lw["w_in"])

    q_a, c_kv, kpe128 = _mla_prep(p_mla, cos, sin, lw["qa_g"], lw["kva_g"], lw["w_uq"], t)
    c_kv3 = c_kv.reshape(b, t, MLA_KV_LORA)
    kpe3 = kpe128.reshape(b, t, LANES)
    if past is None:
        ckv_all, kpe_all = c_kv3, kpe3
    else:
        kpe_past = jnp.pad(past[1], ((0, 0), (0, 0), (ROPE_LO, LANES - ROPE_HI)))
        ckv_all = jnp.concatenate([past[0], c_kv3], 1)
        kpe_all = jnp.concatenate([kpe_past, kpe3], 1)
    if l_pad > l_len:
        ckv_all = jnp.pad(ckv_all, ((0, 0), (0, l_pad - l_len), (0, 0)))
        kpe_all = jnp.pad(kpe_all, ((0, 0), (0, l_pad - l_len), (0, 0)))
    k_a, v_a = _mla_kv(ckv_all.reshape(b * l_pad, MLA_KV_LORA), kpe_all.reshape(b * l_pad, LANES), lw["w_ukv"])
    o_a = _flash(q_a.reshape(b, t, -1), k_a.reshape(b, l_pad, -1), v_a.reshape(b, l_pad, -1),
                 causal="chunk", scale=(MLA_NOPE + MLA_ROPE) ** -0.5, packed_qk=False, kv_len=l_len)

    small3 = p_small.reshape(b, t, LANES)
    p_gdn3 = p_gdn.reshape(b, t, IN_GDN_W)
    if past is None:
        conv_past = jnp.zeros((b, CONV_W - 1, GDN_CONV_DIM), F32)
        s_past = jnp.zeros((b, GDN_HEADS, GDN_DK, GDN_DV), F32)
    else:
        s_past, conv_past = past[5], past[6]
    conv0 = jnp.pad(conv_past, ((0, 0), (SUBLANES - (CONV_W - 1), 0), (0, 0)))
    o_b, s_new = _gdn(p_gdn3, small3, lw["conv_w"], lw["gate"], lw["gdn_g"], conv0, s_past, min(CHUNK, t))
    conv_new = jnp.concatenate([conv_past, p_gdn3[:, -(CONV_W - 1):, :GDN_CONV_DIM]], 1)[:, -(CONV_W - 1):]

    if past is None:
        logf_past, fk_all, fv_all = None, p_fk.reshape(b, t, FOX_W), p_fv.reshape(b, t, FOX_W)
    else:
        logf_past = jnp.pad(past[4], ((0, 0), (0, 0), (FF_LANE, LANES - FF_LANE - FOX_HEADS)))
        fk_all = jnp.concatenate([past[2].reshape(b, p_len, FOX_W), p_fk.reshape(b, t, FOX_W)], 1)
        fv_all = jnp.concatenate([past[3].reshape(b, p_len, FOX_W), p_fv.reshape(b, t, FOX_W)], 1)
    if l_pad > l_len:
        fk_all = jnp.pad(fk_all, ((0, 0), (0, l_pad - l_len), (0, 0)))
        fv_all = jnp.pad(fv_all, ((0, 0), (0, l_pad - l_len), (0, 0)))
    logf128, fcum, fcum_t = _fox_scan(small3, lw["fox_bias"], logf_past, l_pad)
    o_c = _flash(p_fq.reshape(b, t, FOX_W), fk_all, fv_all, causal="frame", scale=FOX_HD ** -0.5,
                 packed_qk=True, kv_len=l_len, fq=fcum, fk=fcum_t)

    x1, qm = _out_ln(o_a.reshape(m, MLA_W), o_b.reshape(m, GDN_V), o_c.reshape(m, FOX_W), x2d, lw["w_out"],
                     lw["ln_g"][0], lw["ln_b"][0], lw["w_xq"], alpha)
    x2 = _mem_attn(qm.reshape(b, t, MEM_W), mem_k, mem_v, x1.reshape(b, t, d), lw["w_xo"],
                   lw["ln_g"][1], lw["ln_b"][1], alpha)
    x3_new = _ffn(x2.reshape(m, d), lw["w_ff1"], lw["w_ff2"], lw["ln_g"][2], lw["ln_b"][2], alpha)

    entries = (c_kv3, kpe3[..., ROPE_LO:ROPE_HI], p_fk.reshape(b, t, FOX_HEADS, FOX_HD),
               p_fv.reshape(b, t, FOX_HEADS, FOX_HD), logf128[..., FF_LANE:FF_LANE + FOX_HEADS], s_new, conv_new)
    return x3_new.reshape(b, t, d), entries


def _lane_block(vals, lane):
    return jnp.pad(vals.astype(F32), ((0, 0), (lane, LANES - lane - vals.shape[-1])))


def kernel(x_prompt, x_sample, cache_mla_ckv, cache_mla_kpe, cache_fox_k, cache_fox_v, cache_fox_logf, state_gdn, state_gdn_conv, cache_mem_k, cache_mem_v, mem_prompt, ln_in_g, ln_in_b, w_in, qa_g, kva_g, w_uq, w_ukv, gdn_conv_w, gdn_a_log, gdn_dt_bias, gdn_norm_g, fox_bf, w_out, w_xq, w_mk, w_mv, w_xo, w_ff1, w_ff2, ln_g, ln_b):
    depth = w_in.shape[0]
    alpha = (2 * depth) ** 0.25
    bp, tp, d = x_prompt.shape
    bs, ts, _ = x_sample.shape
    p_len = cache_mla_ckv.shape[2]

    w_in_p = _pad_w_in(w_in)
    w_uq_p = _pad_w_uq(w_uq)
    w_ukv_p = _pad_w_ukv(w_ukv)
    w_mkv = jnp.concatenate([w_mk, w_mv], -1).astype(BF16)
    gate = jnp.stack([_lane_block(gdn_a_log, GA_LANE), _lane_block(gdn_dt_bias, GA_LANE)], 1)
    fox_bias = _lane_block(fox_bf, FF_LANE)[:, None, :]
    bf = lambda a: a.astype(BF16)
    w_out_b, w_xq_b, w_xo_b, w_ff1_b, w_ff2_b = bf(w_out), bf(w_xq), bf(w_xo), bf(w_ff1), bf(w_ff2)

    tabs_p = _rope_tables(tp, 0)
    tabs_s = _rope_tables(ts, p_len)

    xp = _ln_in(x_prompt.reshape(bp * tp, d), ln_in_g, ln_in_b).reshape(bp, tp, d)
    xs = _ln_in(x_sample.reshape(bs * ts, d), ln_in_g, ln_in_b).reshape(bs, ts, d)
    mem2d = mem_prompt.reshape(bp * N_MEM, d)

    p_new, p_mem, s_new = [], [], []
    for l in range(depth):
        lw = dict(w_in=w_in_p[l], qa_g=qa_g[l], kva_g=kva_g[l], w_uq=w_uq_p[l], w_ukv=w_ukv_p[l],
                  conv_w=gdn_conv_w[l], gate=gate[l], gdn_g=gdn_norm_g[l], fox_bias=fox_bias[l],
                  w_out=w_out_b[l], w_xq=w_xq_b[l], w_xo=w_xo_b[l], w_ff1=w_ff1_b[l], w_ff2=w_ff2_b[l],
                  ln_g=ln_g[l], ln_b=ln_b[l])
        mk, mv = _mem_kv(mem2d, w_mkv[l])
        mk3 = mk.reshape(bp, N_MEM, MEM_W)
        mv3 = mv.reshape(bp, N_MEM, MEM_W)
        xp, ent_p = _encoder_layer(xp, None, mk3, mv3, lw, tabs_p, alpha)
        p_new.append(ent_p)
        p_mem.append((mk.reshape(bp, N_MEM, MEM_HEADS, MEM_HD), mv.reshape(bp, N_MEM, MEM_HEADS, MEM_HD)))
        past = (cache_mla_ckv[l], cache_mla_kpe[l], cache_fox_k[l], cache_fox_v[l], cache_fox_logf[l],
                state_gdn[l], state_gdn_conv[l])
        xs, ent_s = _encoder_layer(xs, past, cache_mem_k[l].reshape(bs, N_MEM, MEM_W),
                                   cache_mem_v[l].reshape(bs, N_MEM, MEM_W), lw, tabs_s, alpha)
        s_new.append(ent_s)

    stack = lambda entries, i: jnp.stack([e[i] for e in entries])
    return (xp, xs,
            stack(p_new, 0), stack(p_new, 1), stack(p_new, 2), stack(p_new, 3), stack(p_new, 4),
            stack(p_new, 5), stack(p_new, 6), stack(p_mem, 0), stack(p_mem, 1),
            stack(s_new, 0), stack(s_new, 1), stack(s_new, 2), stack(s_new, 3), stack(s_new, 4),
            stack(s_new, 5), stack(s_new, 6))
```

```python
import functools
import math

import jax
import jax.numpy as jnp
from jax import lax
from jax.experimental import pallas as pl
from jax.experimental.pallas import tpu as pltpu

F32 = jnp.float32
BF16 = jnp.bfloat16

D_MODEL = 1024
CHUNK = 64
MLA_HEADS = 4
MLA_NOPE = 64
MLA_ROPE = 32
MLA_V = 64
MLA_Q_LORA = 384
MLA_KV_LORA = 256
ROPE_THETA = 10000.0
GDN_HEADS = 4
GDN_DK = 128
GDN_DV = 128
CONV_W = 4
FOX_HEADS = 4
FOX_HD = 64
N_MEM = 256
MEM_HEADS = 4
MEM_HD = 128
D_FF = 4 * D_MODEL
LN_EPS = 1e-5
RMS_EPS = 1e-6
L2_EPS = 1e-6

GDN_QK = GDN_HEADS * GDN_DK
GDN_V = GDN_HEADS * GDN_DV
GDN_CONV_DIM = 2 * GDN_QK + GDN_V
FOX_W = FOX_HEADS * FOX_HD
MLA_W = MLA_HEADS * MLA_V
MEM_W = MEM_HEADS * MEM_HD

LANES = 128
SUBLANES = 8
ROW_TILE = 512
ATTN_BLOCK = 512
FLASH_ROW_SPLIT = 2
SCAN_BLOCK = 256
VMEM_LIMIT = 56 * 1024 * 1024
NEG_BIG = -1e30
LOG2E = 1.0 / math.log(2.0)

FF_LANE = 0
GA_LANE = 4
GB_LANE = 8
ROPE_LO = MLA_NOPE
ROPE_HI = MLA_NOPE + MLA_ROPE


def _params(*sem):
    return pltpu.CompilerParams(dimension_semantics=sem, vmem_limit_bytes=VMEM_LIMIT)


def _resident(shape):
    nd = len(shape)
    return pl.BlockSpec(shape, lambda *_: (0,) * nd)


def _layer_norm(y, g, b):
    mu = jnp.mean(y, axis=-1, keepdims=True)
    yc = y - mu
    var = jnp.mean(yc * yc, axis=-1, keepdims=True)
    return yc * lax.rsqrt(var + LN_EPS) * g + b


def _sigmoid(x):
    return 1.0 / (1.0 + jnp.exp(-x))


def _softplus(x):
    return jnp.maximum(x, 0.0) + jnp.log1p(jnp.exp(-jnp.abs(x)))


def _dot(a, b):
    return jnp.dot(a, b, preferred_element_type=F32)


def _dot_nt(a, b):
    return lax.dot_general(a, b, (((1,), (1,)), ((), ())), preferred_element_type=F32)


def _split3(x):
    hi = x.astype(BF16)
    r1 = x - hi.astype(F32)
    mid = r1.astype(BF16)
    lo = (r1 - mid.astype(F32)).astype(BF16)
    return hi, mid, lo


def _cumsum_rows(tri, x):
    hi, mid, lo = _split3(x)
    return _dot(tri, hi) + _dot(tri, mid) + _dot(tri, lo)


def _tri(n):
    r = lax.broadcasted_iota(jnp.int32, (n, n), 0)
    c = lax.broadcasted_iota(jnp.int32, (n, n), 1)
    return (r >= c).astype(BF16)


def _rope_table_kernel(cos_ref, sin_ref, *, start):
    shape = cos_ref.shape
    lane = lax.broadcasted_iota(jnp.int32, shape, 1)
    row = lax.broadcasted_iota(jnp.int32, shape, 0)
    half = MLA_ROPE // 2
    idx = ((lane - ROPE_LO) & (half - 1)).astype(F32)
    inv = jnp.exp(idx * (-(2.0 / MLA_ROPE) * math.log(ROPE_THETA)))
    ang = (row + start).astype(F32) * inv
    in_rope = (lane >= ROPE_LO) & (lane < ROPE_HI)
    cos_ref[...] = jnp.where(in_rope, jnp.cos(ang), 1.0)
    sin_ref[...] = jnp.where(in_rope, jnp.sin(ang), 0.0)


def _rope_tables(t, start):
    shp = jax.ShapeDtypeStruct((t, LANES), F32)
    return pl.pallas_call(functools.partial(_rope_table_kernel, start=start),
                          out_shape=(shp, shp), name="rope_tables")()


def _ln_kernel(x_ref, g_ref, b_ref, o_ref):
    o_ref[...] = _layer_norm(x_ref[...], g_ref[...], b_ref[...])


def _ln_in(x2d, g, b):
    m, d = x2d.shape
    tm = min(ROW_TILE, m)
    return pl.pallas_call(
        _ln_kernel, grid=(m // tm,),
        in_specs=[pl.BlockSpec((tm, d), lambda i: (i, 0)), _resident((1, d)), _resident((1, d))],
        out_specs=pl.BlockSpec((tm, d), lambda i: (i, 0)),
        out_shape=jax.ShapeDtypeStruct((m, d), F32),
        compiler_params=_params("parallel"), name="ln_in")(x2d, g.reshape(1, d), b.reshape(1, d))


IN_MLA_W = MLA_Q_LORA + MLA_KV_LORA + 2 * LANES
IN_GDN_W = GDN_CONV_DIM + GDN_V
IN_PAD_COLS = (IN_MLA_W, IN_GDN_W, LANES, FOX_W, FOX_W, FOX_W)
IN_PAD_OFFS = tuple(sum(IN_PAD_COLS[:n]) for n in range(len(IN_PAD_COLS) + 1))


def _rms(x, g):
    return x * lax.rsqrt(jnp.mean(x * x, axis=-1, keepdims=True) + RMS_EPS) * g


def _mla_keys_values(ckv, kpe, w_ref, k_ref, v_ref):
    kv = _dot(ckv.astype(BF16), w_ref[...])
    kw = MLA_HEADS * LANES
    for h in range(MLA_HEADS):
        k_ref[:, h * LANES:(h + 1) * LANES] = (kv[:, h * LANES:(h + 1) * LANES] + kpe).astype(BF16)
    v_ref[...] = kv[:, kw:].astype(BF16)


def _in_proj_kernel(x_ref, w_ref, cos_ref, sin_ref, qg_ref, kg_ref, wq_ref, *rest, with_kv):
    if with_kv:
        wkv_ref, q_ref, ckv_ref, kpe_ref, k_ref, v_ref = rest[:6]
    else:
        q_ref, ckv_ref, kpe_ref = rest[:3]
    o_refs = rest[-(len(IN_PAD_COLS) - 1):]
    xb = x_ref[...].astype(BF16)
    p = _dot(xb, w_ref[:, :IN_MLA_W])
    cos = cos_ref[...]
    sin = sin_ref[...]
    c_q = _rms(p[:, :MLA_Q_LORA], qg_ref[...]).astype(BF16)
    qq = _dot(c_q, wq_ref[...])
    hw = MLA_HEADS * LANES
    for h in range(MLA_HEADS):
        a = qq[:, h * LANES:(h + 1) * LANES]
        b = qq[:, hw + h * LANES:hw + (h + 1) * LANES]
        q_ref[:, h * LANES:(h + 1) * LANES] = (a * cos + b * sin).astype(BF16)
    lo = MLA_Q_LORA
    ckv = _rms(p[:, lo:lo + MLA_KV_LORA], kg_ref[...])
    lo += MLA_KV_LORA
    kpe = p[:, lo:lo + LANES] * cos + p[:, lo + LANES:lo + 2 * LANES] * sin
    ckv_ref[...] = ckv
    kpe_ref[...] = kpe
    if with_kv:
        _mla_keys_values(ckv, kpe, wkv_ref, k_ref, v_ref)
    for o_ref, lo, hi in zip(o_refs, IN_PAD_OFFS[1:-1], IN_PAD_OFFS[2:]):
        o_ref[...] = _dot(xb, w_ref[:, lo:hi])


def _in_proj(x2d, w_pad, cos, sin, qa_g, kva_g, w_uq2, w_ukv2, t):
    m, d = x2d.shape
    tm = min(ROW_TILE, m)
    if t < tm:
        cos, sin = jnp.tile(cos, (tm // t, 1)), jnp.tile(sin, (tm // t, 1))
    nt = max(t // tm, 1)
    with_kv = w_ukv2 is not None
    row = lambda w: pl.BlockSpec((tm, w), lambda i: (i, 0))
    tab = pl.BlockSpec((tm, LANES), lambda i: (i % nt, 0))
    in_specs = [row(d), _resident(w_pad.shape), tab, tab, _resident((1, MLA_Q_LORA)), _resident((1, MLA_KV_LORA)),
                _resident(w_uq2.shape)]
    args = [x2d, w_pad, cos, sin, qa_g.reshape(1, -1), kva_g.reshape(1, -1), w_uq2]
    outs = [(MLA_HEADS * LANES, BF16), (MLA_KV_LORA, F32), (LANES, F32)]
    if with_kv:
        in_specs.append(_resident(w_ukv2.shape))
        args.append(w_ukv2)
        outs += [(MLA_HEADS * LANES, BF16), (MLA_W, BF16)]
    outs += [(w, F32) for w in IN_PAD_COLS[1:]]
    return pl.pallas_call(
        functools.partial(_in_proj_kernel, with_kv=with_kv), grid=(m // tm,),
        in_specs=in_specs,
        out_specs=[row(w) for w, _ in outs],
        out_shape=[jax.ShapeDtypeStruct((m, w), dt) for w, dt in outs],
        compiler_params=_params("parallel"), name="in_proj")(*args)


def _pad_w_in(w_in):
    offs = [0]
    for s in (MLA_Q_LORA, MLA_KV_LORA, MLA_ROPE, GDN_QK, GDN_QK, GDN_V, GDN_V, GDN_HEADS, GDN_HEADS,
              FOX_W, FOX_W, FOX_W, FOX_HEADS):
        offs.append(offs[-1] + s)
    cq, ckv, kpe, gq, gk, gv, gz, ga, gb, fq, fk, fv, ff = [w_in[..., a:b] for a, b in zip(offs[:-1], offs[1:])]
    half = MLA_ROPE // 2
    kpe_rot = jnp.concatenate([-kpe[..., half:], kpe[..., :half]], -1)

    def z(n):
        return jnp.zeros(w_in.shape[:-1] + (n,), w_in.dtype)

    def rope_block(w):
        return jnp.concatenate([z(ROPE_LO), w, z(LANES - ROPE_HI)], -1)

    small = jnp.concatenate([ff, ga, gb, z(LANES - 3 * GDN_HEADS)], -1)
    return jnp.concatenate([cq, ckv, rope_block(kpe), rope_block(kpe_rot), gq, gk, gv, gz, small, fq, fk, fv],
                           -1).astype(BF16)


def _pad_w_uq(w_uq):
    half = MLA_ROPE // 2
    w = w_uq.reshape(w_uq.shape[:-1] + (MLA_HEADS, MLA_NOPE + MLA_ROPE))
    nope, rp = w[..., :MLA_NOPE], w[..., MLA_NOPE:]
    rot = jnp.concatenate([-rp[..., half:], rp[..., :half]], -1)
    zpad = jnp.zeros(w.shape[:-1] + (LANES - ROPE_HI,), w.dtype)
    plain = jnp.concatenate([nope, rp, zpad], -1)
    rotated = jnp.concatenate([jnp.zeros_like(nope), rot, zpad], -1)
    flat = lambda a: a.reshape(a.shape[:-2] + (MLA_HEADS * LANES,))
    return jnp.concatenate([flat(plain), flat(rotated)], -1).astype(BF16)


def _mla_kv_kernel(ckv_ref, kpe_ref, w_ref, k_ref, v_ref):
    _mla_keys_values(ckv_ref[...], kpe_ref[...], w_ref, k_ref, v_ref)


def _mla_kv(ckv_all, kpe_all, w_ukv2):
    r = ckv_all.shape[0]
    tm = min(ROW_TILE, r)
    row = lambda w: pl.BlockSpec((tm, w), lambda i: (i, 0))
    return pl.pallas_call(
        _mla_kv_kernel, grid=(r // tm,),
        in_specs=[row(MLA_KV_LORA), row(LANES), _resident(w_ukv2.shape)],
        out_specs=[row(MLA_HEADS * LANES), row(MLA_W)],
        out_shape=[jax.ShapeDtypeStruct((r, MLA_HEADS * LANES), BF16), jax.ShapeDtypeStruct((r, MLA_W), BF16)],
        compiler_params=_params("parallel"), name="mla_kv")(ckv_all, kpe_all, w_ukv2)


def _pad_w_ukv(w_ukv):
    w = w_ukv.reshape(w_ukv.shape[:-1] + (MLA_HEADS, MLA_NOPE + MLA_V))
    nope, v = w[..., :MLA_NOPE], w[..., MLA_NOPE:]
    kpad = jnp.concatenate([nope, jnp.zeros(nope.shape[:-1] + (LANES - MLA_NOPE,), w.dtype)], -1)
    flat = lambda a: a.reshape(a.shape[:-2] + (-1,))
    return jnp.concatenate([flat(kpad), flat(v)], -1).astype(BF16)


def _flash_kernel(*refs, tq, tk, q_start, kv_len, kv_pad, causal, scale, packed_qk, has_f, heads):
    if has_f:
        q_ref, k_ref, v_ref, fq_ref, fk_ref, o_ref, m_sc, acc_sc, fq_sc = refs
    else:
        q_ref, k_ref, v_ref, o_ref, m_sc, acc_sc = refs
    i = pl.program_id(1)
    j = pl.program_id(2)
    nk = pl.num_programs(2)

    @pl.when(j == 0)
    def _():
        m_sc[...] = jnp.full(m_sc.shape, NEG_BIG, F32)
        acc_sc[...] = jnp.zeros(acc_sc.shape, F32)
        if has_f:
            for h in range(heads):
                fq_sc[h] = jnp.broadcast_to(fq_ref[0, :, FF_LANE + h:FF_LANE + h + 1], (tq, LANES))

    qmin = q_start + i * tq
    qmax = qmin + tq - 1
    kmin = j * tk
    kmax = kmin + tk - 1
    if causal == "frame":
        any_vis = kmin <= qmax
        all_vis = kmax <= qmin
    else:
        any_vis = (kmin // CHUNK) <= (qmax // CHUNK)
        all_vis = (kmax // CHUNK) <= (qmin // CHUNK)
    if kv_len < kv_pad:
        all_vis = jnp.logical_and(all_vis, kmax < kv_len)

    fold = math.log2(scale) == round(math.log2(scale))
    half_lane = lax.broadcasted_iota(jnp.int32, (1, LANES), 1) < (LANES // 2)

    def body(masked):
        if masked:
            qpos = qmin + lax.broadcasted_iota(jnp.int32, (tq, tk), 0)
            kpos = kmin + lax.broadcasted_iota(jnp.int32, (tq, tk), 1)
            if causal == "frame":
                vis = kpos <= qpos
            else:
                vis = (kpos // CHUNK) <= (qpos // CHUNK)
            if kv_len < kv_pad:
                vis = jnp.logical_and(vis, kpos < kv_len)
        nsplit = FLASH_ROW_SPLIT if tq % (FLASH_ROW_SPLIT * LANES) == 0 else 1
        rs = tq // nsplit
        units = [(h, r) for h in range(heads) for r in range(nsplit)]
        c = LOG2E if fold else LOG2E * scale
        k_cache, v_cache = {}, {}

        def k_block(b0):
            if b0 not in k_cache:
                k_cache[b0] = k_ref[0, :, b0 * LANES:(b0 + 1) * LANES].astype(BF16)
            return k_cache[b0]

        def v_block(h):
            if h not in v_cache:
                vpair = v_ref[0, :, (h // 2) * LANES:(h // 2 + 1) * LANES].astype(BF16)
                mine = half_lane if h % 2 == 0 else jnp.logical_not(half_lane)
                v_cache[h] = jnp.where(mine, vpair, jnp.ones_like(vpair))
            return v_cache[h]

        def scores(h, r):
            rows = slice(r * rs, (r + 1) * rs)
            b0 = h // 2 if packed_qk else h
            qh = q_ref[0, rows, b0 * LANES:(b0 + 1) * LANES]
            if packed_qk:
                qh = jnp.where(half_lane if h % 2 == 0 else jnp.logical_not(half_lane), qh, 0.0)
            if fold:
                qh = qh * scale
            return _dot_nt(qh.astype(BF16), k_block(b0))

        def softmax_part(h, r, s):
            rows = slice(r * rs, (r + 1) * rs)
            if has_f:
                s = s - fk_ref[0, FF_LANE + h:FF_LANE + h + 1, :]
            if masked:
                s = jnp.where(vis[rows], s, NEG_BIG)
            m_old = m_sc[h, rows]
            row_max = jnp.max(s, axis=-1, keepdims=True)
            if has_f:
                fq_b = fq_sc[h, rows]
                m_new = jnp.maximum(m_old, row_max + fq_b)
                shift = m_new - fq_b
            else:
                m_new = jnp.maximum(m_old, row_max)
                shift = m_new
            m_sc[h, rows] = m_new
            alpha = jnp.exp2((m_old - m_new) * c)
            e = jnp.concatenate([jnp.exp2((s[:, b * LANES:(b + 1) * LANES] - shift) * c)
                                 for b in range(tk // LANES)], -1)
            return alpha, e.astype(BF16)

        def weighted_values(h, r, alpha, e):
            rows = slice(r * rs, (r + 1) * rs)
            acc_sc[h, rows] = acc_sc[h, rows] * alpha + _dot(e, v_block(h))

        s_q, e_q = {}, {}
        for n in range(len(units) + 2):
            if n < len(units):
                s_q[n] = scores(*units[n])
            if 0 <= n - 1 < len(units):
                e_q[n - 1] = softmax_part(*units[n - 1], s_q.pop(n - 1))
            if 0 <= n - 2 < len(units):
                weighted_values(*units[n - 2], *e_q.pop(n - 2))

    @pl.when(all_vis)
    def _():
        body(False)

    @pl.when(jnp.logical_and(any_vis, jnp.logical_not(all_vis)))
    def _():
        body(True)

    @pl.when(j == nk - 1)
    def _():
        for pair in range(heads // 2):
            a_lo = acc_sc[2 * pair]
            a_hi = acc_sc[2 * pair + 1]
            lo = a_lo / pltpu.roll(a_lo, LANES // 2, 1)
            hi = a_hi / pltpu.roll(a_hi, LANES // 2, 1)
            o_ref[0, :, pair * LANES:(pair + 1) * LANES] = jnp.where(half_lane, lo, hi).astype(o_ref.dtype)


def _flash(q, k, v, *, causal, scale, packed_qk, kv_len, fq=None, fk=None, tq=None, tk=None):
    b, t, wq = q.shape
    lp = k.shape[1]
    heads = v.shape[2] // (LANES // 2)
    tq = tq or min(ATTN_BLOCK, t)
    tk = tk or (min(ATTN_BLOCK, lp) if t >= ATTN_BLOCK else lp)
    q_start = kv_len - t
    nq, nk = t // tq, lp // tk

    def last_block(i):
        qmax = q_start + i * tq + tq - 1
        last = qmax if causal == "frame" else (qmax // CHUNK) * CHUNK + CHUNK - 1
        return jnp.minimum(last // tk, nk - 1)

    kv_map = lambda bi, i, j: (bi, jnp.minimum(j, last_block(i)), 0)
    in_specs = [pl.BlockSpec((1, tq, wq), lambda bi, i, j: (bi, i, 0)),
                pl.BlockSpec((1, tk, wq), kv_map),
                pl.BlockSpec((1, tk, v.shape[2]), kv_map)]
    args = [q, k, v]
    has_f = fq is not None
    if has_f:
        in_specs += [pl.BlockSpec((1, tq, LANES), lambda bi, i, j: (bi, i + q_start // tq, 0)),
                     pl.BlockSpec((1, SUBLANES, tk), lambda bi, i, j: (bi, 0, jnp.minimum(j, last_block(i))))]
        args += [fq, fk]
        assert q_start % tq == 0
    kern = functools.partial(_flash_kernel, tq=tq, tk=tk, q_start=q_start, kv_len=kv_len, kv_pad=lp,
                             causal=causal, scale=scale, packed_qk=packed_qk, has_f=has_f, heads=heads)
    return pl.pallas_call(
        kern, grid=(b, nq, nk), in_specs=in_specs,
        out_specs=pl.BlockSpec((1, tq, v.shape[2]), lambda bi, i, j: (bi, i, 0)),
        out_shape=jax.ShapeDtypeStruct((b, t, v.shape[2]), BF16),
        scratch_shapes=[pltpu.VMEM((heads, tq, LANES), F32)] * (3 if has_f else 2),
        compiler_params=_params("parallel", "parallel", "arbitrary"),
        name="flash_" + causal)(*args)


def _fox_scan_kernel(*refs, p_len, t_len, l_pad):
    if p_len:
        small_ref, bias_ref, past_ref, logf_ref, fc_ref, fct_ref = refs
    else:
        small_ref, bias_ref, logf_ref, fc_ref, fct_ref = refs
    x = small_ref[0] + bias_ref[...]
    logf = jnp.minimum(x, 0.0) - jnp.log1p(jnp.exp(-jnp.abs(x)))
    logf_ref[0] = logf
    bs = SCAN_BLOCK
    tri = _tri(bs)
    carry = jnp.zeros((1, LANES), F32)
    for blk in range(l_pad // bs):
        lo = blk * bs
        if lo + bs <= p_len:
            xb = past_ref[0, lo:lo + bs, :]
        elif lo >= p_len and lo + bs <= p_len + t_len:
            xb = logf[lo - p_len:lo - p_len + bs]
        else:
            assert lo == p_len and t_len < bs
            xb = jnp.concatenate([logf, jnp.zeros((bs - t_len, LANES), F32)], 0)
        cs = _cumsum_rows(tri, xb) + carry
        carry = cs[bs - 1:bs, :]
        fc_ref[0, lo:lo + bs, :] = cs
        fct_ref[0, :, lo:lo + bs] = cs.T[:SUBLANES, :]


def _fox_scan(small3, bias128, past128, l_pad):
    b, t, _ = small3.shape
    p_len = 0 if past128 is None else past128.shape[1]
    per_b = lambda r, w: pl.BlockSpec((1, r, w), lambda bi: (bi, 0, 0))
    in_specs = [per_b(t, LANES), _resident((1, LANES))]
    args = [small3, bias128]
    if p_len:
        in_specs.append(per_b(p_len, LANES))
        args.append(past128)
    return pl.pallas_call(
        functools.partial(_fox_scan_kernel, p_len=p_len, t_len=t, l_pad=l_pad), grid=(b,),
        in_specs=in_specs,
        out_specs=[per_b(t, LANES), per_b(l_pad, LANES), per_b(SUBLANES, l_pad)],
        out_shape=[jax.ShapeDtypeStruct((b, t, LANES), F32), jax.ShapeDtypeStruct((b, l_pad, LANES), F32),
                   jax.ShapeDtypeStruct((b, SUBLANES, l_pad), F32)],
        compiler_params=_params("parallel"), name="fox_scan")(*args)


GDN_BLOCK = 256


def _gdn_kernel(qkv_ref, z_ref, small_ref, convw_ref, gate_ref, ng_ref, conv0_ref, s0_ref,
                o_ref, sout_ref, prev_sc, s_sc, *, blk, c_len):
    step = pl.program_id(1)
    nsteps = pl.num_programs(1)
    nsub = blk // c_len

    @pl.when(step == 0)
    def _():
        prev_sc[...] = conv0_ref[0]
        s_sc[...] = s0_ref[0]

    def causal_conv(rows_in):
        out = rows_in * convw_ref[CONV_W - 1:CONV_W, :]
        for s in range(1, CONV_W):
            out = out + pltpu.roll(rows_in, s, 0) * convw_ref[CONV_W - 1 - s:CONV_W - s, :]
        return out

    x = qkv_ref[0]
    first = causal_conv(jnp.concatenate([prev_sc[...], x[:SUBLANES]], 0))[SUBLANES:]
    y = jnp.concatenate([first, causal_conv(x)[SUBLANES:]], 0)
    prev_sc[...] = x[blk - SUBLANES:]
    y = y * _sigmoid(y)

    ri = lax.broadcasted_iota(jnp.int32, (blk, blk), 0)
    ci = lax.broadcasted_iota(jnp.int32, (blk, blk), 1)
    same = (ri // c_len) == (ci // c_len)
    incl = jnp.logical_and(ri >= ci, same)
    strict = jnp.logical_and(ri > ci, same)

    small = small_ref[0]
    g128 = -jnp.exp(gate_ref[0:1, :]) * _softplus(small + gate_ref[1:2, :])
    beta128 = _sigmoid(small)
    gcum = _cumsum_rows(incl.astype(BF16), g128)
    gcum_t = gcum.T
    eg128 = jnp.exp(gcum)
    glast128 = jnp.concatenate(
        [jnp.broadcast_to(gcum[(c + 1) * c_len - 1:(c + 1) * c_len, :], (c_len, LANES)) for c in range(nsub)], 0)
    ekd128 = jnp.exp(glast128 - gcum)
    nlev = int(math.log2(c_len))
    assert 2 ** nlev == c_len

    heads = range(GDN_HEADS)
    kh_l, kb_l, a_l, decay_l, rhs_l, qcb_l, qd_l, kdt_l = [], [], [], [], [], [], [], []
    for h in heads:
        qh = y[:, h * GDN_DK:(h + 1) * GDN_DK]
        kh = y[:, GDN_QK + h * GDN_DK:GDN_QK + (h + 1) * GDN_DK]
        vh = y[:, 2 * GDN_QK + h * GDN_DV:2 * GDN_QK + (h + 1) * GDN_DV]
        qh = qh * lax.rsqrt(jnp.sum(qh * qh, axis=-1, keepdims=True) + L2_EPS)
        kh = kh * lax.rsqrt(jnp.sum(kh * kh, axis=-1, keepdims=True) + L2_EPS)
        lane = GA_LANE + h
        g_col = gcum[:, lane:lane + 1]
        g_row = gcum_t[lane:lane + 1, :]
        eg_col = eg128[:, lane:lane + 1]
        beta = beta128[:, GB_LANE + h:GB_LANE + h + 1]
        decay = jnp.exp(jnp.where(incl, g_col - g_row, NEG_BIG))
        kb = kh.astype(BF16)
        a_l.append(jnp.where(strict, beta * _dot_nt(kb, kb) * decay, 0.0))
        rhs_l.append(jnp.concatenate([vh * beta, kh * (beta * eg_col)], -1))
        qc = qh * (GDN_DK ** -0.5)
        qcb_l.append(qc.astype(BF16))
        qd_l.append((qc * eg_col).astype(BF16))
        kdt_l.append((kh * ekd128[:, lane:lane + 1]).T.astype(BF16))
        kb_l.append(kb)
        decay_l.append(decay)

    tm_l = [-a for a in a_l]
    pb_l = [a.astype(BF16) for a in a_l]
    for _ in range(1, nlev):
        pw_l = [_dot(pb, pb) for pb in pb_l]
        pb_l = [pw.astype(BF16) for pw in pw_l]
        tm_l = [tm + pw + _dot(tm.astype(BF16), pb) for tm, pw, pb in zip(tm_l, pw_l, pb_l)]
    sol_l = [rhs + _dot(tm.astype(BF16), rhs.astype(BF16)) for tm, rhs in zip(tm_l, rhs_l)]
    u_l = [sol[:, :GDN_DV] for sol in sol_l]
    wb_l = [sol[:, GDN_DV:].astype(BF16) for sol in sol_l]
    qk_l = [(_dot_nt(qcb, kb) * decay).astype(BF16) for qcb, kb, decay in zip(qcb_l, kb_l, decay_l)]

    s_l = [s_sc[h] for h in heads]
    v_parts = [[] for _ in heads]
    qs_parts = [[] for _ in heads]
    for c in range(nsub):
        rows = slice(c * c_len, (c + 1) * c_len)
        r_l = [_dot(jnp.concatenate([wb_l[h][rows], qd_l[h][rows]], 0), s_l[h].astype(BF16)) for h in heads]
        for h in heads:
            v_parts[h].append(u_l[h][rows] - r_l[h][:c_len])
            qs_parts[h].append(r_l[h][c_len:])
        s_l = [s_l[h] * eg128[(c + 1) * c_len - 1:(c + 1) * c_len, GA_LANE + h:GA_LANE + h + 1]
               + _dot(kdt_l[h][:, rows], v_parts[h][c].astype(BF16)) for h in heads]
    for h in heads:
        s_sc[h] = s_l[h]
        v_all = v_parts[h][0] if nsub == 1 else jnp.concatenate(v_parts[h], 0)
        qs_all = qs_parts[h][0] if nsub == 1 else jnp.concatenate(qs_parts[h], 0)
        o = qs_all + _dot(qk_l[h], v_all.astype(BF16))
        zh = z_ref[0, :, h * GDN_DV:(h + 1) * GDN_DV]
        o = _rms(o, ng_ref[...]) * (zh * _sigmoid(zh))
        o_ref[0, :, h * GDN_DV:(h + 1) * GDN_DV] = o.astype(o_ref.dtype)

    @pl.when(step == nsteps - 1)
    def _():
        sout_ref[0] = s_sc[...]


def _gdn(p_gdn3, small3, conv_w, gate2, norm_g, conv0, s0, c_len):
    b, t, _ = p_gdn3.shape
    blk = min(GDN_BLOCK, t)
    return pl.pallas_call(
        functools.partial(_gdn_kernel, blk=blk, c_len=c_len), grid=(b, t // blk),
        in_specs=[pl.BlockSpec((1, blk, GDN_CONV_DIM), lambda bi, c: (bi, c, 0)),
                  pl.BlockSpec((1, blk, GDN_V), lambda bi, c: (bi, c, GDN_CONV_DIM // GDN_V)),
                  pl.BlockSpec((1, blk, LANES), lambda bi, c: (bi, c, 0)),
                  _resident((CONV_W, GDN_CONV_DIM)), _resident((2, LANES)), _resident((1, GDN_DV)),
                  pl.BlockSpec((1, SUBLANES, GDN_CONV_DIM), lambda bi, c: (bi, 0, 0)),
                  pl.BlockSpec((1, GDN_HEADS, GDN_DK, GDN_DV), lambda bi, c: (bi, 0, 0, 0))],
        out_specs=[pl.BlockSpec((1, blk, GDN_V), lambda bi, c: (bi, c, 0)),
                   pl.BlockSpec((1, GDN_HEADS, GDN_DK, GDN_DV), lambda bi, c: (bi, 0, 0, 0))],
        out_shape=[jax.ShapeDtypeStruct((b, t, GDN_V), BF16),
                   jax.ShapeDtypeStruct((b, GDN_HEADS, GDN_DK, GDN_DV), F32)],
        scratch_shapes=[pltpu.VMEM((SUBLANES, GDN_CONV_DIM), F32), pltpu.VMEM((GDN_HEADS, GDN_DK, GDN_DV), F32)],
        compiler_params=_params("parallel", "arbitrary"), name="gdn")(
            p_gdn3, p_gdn3, small3, conv_w, gate2, norm_g.reshape(1, -1), conv0, s0)


def _out_ln_kernel(oa_ref, ob_ref, oc_ref, x_ref, w_ref, g_ref, b_ref, wq_ref, x1_ref, qm_ref, *, alpha):
    mix = (_dot(oa_ref[...], w_ref[:MLA_W, :]) + _dot(ob_ref[...], w_ref[MLA_W:MLA_W + GDN_V, :])
           + _dot(oc_ref[...], w_ref[MLA_W + GDN_V:, :]))
    x1 = _layer_norm(alpha * x_ref[...] + mix, g_ref[...], b_ref[...])
    x1_ref[...] = x1
    qm_ref[...] = _dot(x1.astype(BF16), wq_ref[...]).astype(BF16)


def _out_ln(o_a, o_b, o_c, x2d, w_out, g, b, w_xq, alpha):
    m, d = x2d.shape
    tm = min(ROW_TILE, m)
    row = lambda w: pl.BlockSpec((tm, w), lambda i: (i, 0))
    return pl.pallas_call(
        functools.partial(_out_ln_kernel, alpha=alpha), grid=(m // tm,),
        in_specs=[row(MLA_W), row(GDN_V), row(FOX_W), row(d), _resident(w_out.shape), _resident((1, d)),
                  _resident((1, d)), _resident(w_xq.shape)],
        out_specs=[row(d), row(MEM_W)],
        out_shape=[jax.ShapeDtypeStruct((m, d), F32), jax.ShapeDtypeStruct((m, MEM_W), BF16)],
        compiler_params=_params("parallel"), name="out_ln")(
            o_a, o_b, o_c, x2d, w_out, g.reshape(1, d), b.reshape(1, d), w_xq)


def _mem_kv_kernel(m_ref, w_ref, k_ref, v_ref):
    kv = _dot(m_ref[...].astype(BF16), w_ref[...])
    k_ref[...] = kv[:, :MEM_W]
    v_ref[...] = kv[:, MEM_W:]


def _mem_kv(mem2d, w_mkv):
    m, d = mem2d.shape
    tm = min(ROW_TILE, m)
    row = lambda w: pl.BlockSpec((tm, w), lambda i: (i, 0))
    shp = jax.ShapeDtypeStruct((m, MEM_W), F32)
    return pl.pallas_call(
        _mem_kv_kernel, grid=(m // tm,),
        in_specs=[row(d), _resident(w_mkv.shape)], out_specs=[row(MEM_W), row(MEM_W)], out_shape=[shp, shp],
        compiler_params=_params("parallel"), name="mem_kv")(mem2d, w_mkv)


def _mem_attn_kernel(q_ref, k_ref, v_ref, x_ref, w_ref, g_ref, b_ref, o_ref, *, alpha):
    c = (MEM_HD ** -0.5) * LOG2E
    ones = jnp.ones((k_ref.shape[1], LANES), BF16)
    outs = []
    for h in range(MEM_HEADS):
        cols = slice(h * MEM_HD, (h + 1) * MEM_HD)
        s = _dot_nt(q_ref[0, :, cols], k_ref[0, :, cols].astype(BF16))
        e = jnp.exp2((s - jnp.max(s, axis=-1, keepdims=True)) * c).astype(BF16)
        outs.append(_dot(e, v_ref[0, :, cols].astype(BF16)) / _dot(e, ones))
    o = jnp.concatenate(outs, -1).astype(BF16)
    y = alpha * x_ref[0] + _dot(o, w_ref[...])
    o_ref[0] = _layer_norm(y, g_ref[...], b_ref[...])


def _mem_attn(qm3, mk3, mv3, x3, w_xo, g, b, alpha):
    bsz, t, d = x3.shape
    tq = min(ROW_TILE, t)
    nm = mk3.shape[1]
    return pl.pallas_call(
        functools.partial(_mem_attn_kernel, alpha=alpha), grid=(bsz, t // tq),
        in_specs=[pl.BlockSpec((1, tq, MEM_W), lambda bi, i: (bi, i, 0)),
                  pl.BlockSpec((1, nm, MEM_W), lambda bi, i: (bi, 0, 0)),
                  pl.BlockSpec((1, nm, MEM_W), lambda bi, i: (bi, 0, 0)),
                  pl.BlockSpec((1, tq, d), lambda bi, i: (bi, i, 0)),
                  _resident(w_xo.shape), _resident((1, d)), _resident((1, d))],
        out_specs=pl.BlockSpec((1, tq, d), lambda bi, i: (bi, i, 0)),
        out_shape=jax.ShapeDtypeStruct((bsz, t, d), F32),
        compiler_params=_params("parallel", "parallel"), name="mem_attn")(
            qm3, mk3, mv3, x3, w_xo, g.reshape(1, d), b.reshape(1, d))


FFN_CHUNK = 1024


def _ffn_kernel(x_ref, w1_ref, w2_ref, g_ref, b_ref, o_ref, *, alpha):
    x = x_ref[...]
    xb = x.astype(BF16)
    acc = alpha * x
    for c in range(D_FF // FFN_CHUNK):
        cols = slice(c * FFN_CHUNK, (c + 1) * FFN_CHUNK)
        hdn = jnp.maximum(_dot(xb, w1_ref[:, cols]), 0.0)
        acc = acc + _dot((hdn * hdn).astype(BF16), w2_ref[cols, :])
    o_ref[...] = _layer_norm(acc, g_ref[...], b_ref[...])


def _ffn(x2d, w1, w2, g, b, alpha):
    m, d = x2d.shape
    tm = min(ROW_TILE, m)
    row = pl.BlockSpec((tm, d), lambda i: (i, 0))
    return pl.pallas_call(
        functools.partial(_ffn_kernel, alpha=alpha), grid=(m // tm,),
        in_specs=[row, _resident(w1.shape), _resident(w2.shape), _resident((1, d)), _resident((1, d))],
        out_specs=row, out_shape=jax.ShapeDtypeStruct((m, d), F32),
        compiler_params=_params("parallel"), name="ffn")(x2d, w1, w2, g.reshape(1, d), b.reshape(1, d))


def _round_up(n, m):
    return -(-n // m) * m


def _encoder_layer(x3, past, mem_k, mem_v, lw, rope_tabs, alpha):
    b, t, d = x3.shape
    m = b * t
    x2d = x3.reshape(m, d)
    p_len = 0 if past is None else past[0].shape[1]
    l_len = p_len + t
    l_pad = _round_up(l_len, SCAN_BLOCK)
    cos, sin = rope_tabs

    fused_kv = past is None and l_pad == l_len
    proj = _in_proj(x2d, lw["w_in"], cos, sin, lw["qa_g"], lw["kva_g"], lw["w_uq"],
                    lw["w_ukv"] if fused_kv else None, t)
    q_a, c_kv, kpe128 = proj[:3]
    p_gdn, p_small, p_fq, p_fk, p_fv = proj[-5:]

    c_kv3 = c_kv.reshape(b, t, MLA_KV_LORA)
    kpe3 = kpe128.reshape(b, t, LANES)
    if fused_kv:
        k_a, v_a = proj[3:5]
    else:
        ckv_all, kpe_all = c_kv3, kpe3
        if past is not None:
            kpe_past = jnp.pad(past[1], ((0, 0), (0, 0), (ROPE_LO, LANES - ROPE_HI)))
            ckv_all = jnp.concatenate([past[0], c_kv3], 1)
            kpe_all = jnp.concatenate([kpe_past, kpe3], 1)
        if l_pad > l_len:
            ckv_all = jnp.pad(ckv_all, ((0, 0), (0, l_pad - l_len), (0, 0)))
            kpe_all = jnp.pad(kpe_all, ((0, 0), (0, l_pad - l_len), (0, 0)))
        k_a, v_a = _mla_kv(ckv_all.reshape(b * l_pad, MLA_KV_LORA), kpe_all.reshape(b * l_pad, LANES),
                           lw["w_ukv"])
    o_a = _flash(q_a.reshape(b, t, -1), k_a.reshape(b, l_pad, -1), v_a.reshape(b, l_pad, -1),
                 causal="chunk", scale=(MLA_NOPE + MLA_ROPE) ** -0.5, packed_qk=False, kv_len=l_len)

    small3 = p_small.reshape(b, t, LANES)
    p_gdn3 = p_gdn.reshape(b, t, IN_GDN_W)
    if past is None:
        conv_past = jnp.zeros((b, CONV_W - 1, GDN_CONV_DIM), F32)
        s_past = jnp.zeros((b, GDN_HEADS, GDN_DK, GDN_DV), F32)
    else:
        s_past, conv_past = past[5], past[6]
    conv0 = jnp.pad(conv_past, ((0, 0), (SUBLANES - (CONV_W - 1), 0), (0, 0)))
    o_b, s_new = _gdn(p_gdn3, small3, lw["conv_w"], lw["gate"], lw["gdn_g"], conv0, s_past, min(CHUNK, t))
    conv_new = jnp.concatenate([conv_past, p_gdn3[:, -(CONV_W - 1):, :GDN_CONV_DIM]], 1)[:, -(CONV_W - 1):]

    if past is None:
        logf_past, fk_all, fv_all = None, p_fk.reshape(b, t, FOX_W), p_fv.reshape(b, t, FOX_W)
    else:
        logf_past = jnp.pad(past[4], ((0, 0), (0, 0), (FF_LANE, LANES - FF_LANE - FOX_HEADS)))
        fk_all = jnp.concatenate([past[2].reshape(b, p_len, FOX_W), p_fk.reshape(b, t, FOX_W)], 1)
        fv_all = jnp.concatenate([past[3].reshape(b, p_len, FOX_W), p_fv.reshape(b, t, FOX_W)], 1)
    if l_pad > l_len:
        fk_all = jnp.pad(fk_all, ((0, 0), (0, l_pad - l_len), (0, 0)))
        fv_all = jnp.pad(fv_all, ((0, 0), (0, l_pad - l_len), (0, 0)))
    logf128, fcum, fcum_t = _fox_scan(small3, lw["fox_bias"], logf_past, l_pad)
    o_c = _flash(p_fq.reshape(b, t, FOX_W), fk_all, fv_all, causal="frame", scale=FOX_HD ** -0.5,
                 packed_qk=True, kv_len=l_len, fq=fcum, fk=fcum_t)

    x1, qm = _out_ln(o_a.reshape(m, MLA_W), o_b.reshape(m, GDN_V), o_c.reshape(m, FOX_W), x2d, lw["w_out"],
                     lw["ln_g"][0], lw["ln_b"][0], lw["w_xq"], alpha)
    x2 = _mem_attn(qm.reshape(b, t, MEM_W), mem_k, mem_v, x1.reshape(b, t, d), lw["w_xo"],
                   lw["ln_g"][1], lw["ln_b"][1], alpha)
    x3_new = _ffn(x2.reshape(m, d), lw["w_ff1"], lw["w_ff2"], lw["ln_g"][2], lw["ln_b"][2], alpha)

    entries = (c_kv3, kpe3[..., ROPE_LO:ROPE_HI], p_fk.reshape(b, t, FOX_HEADS, FOX_HD),
               p_fv.reshape(b, t, FOX_HEADS, FOX_HD), logf128[..., FF_LANE:FF_LANE + FOX_HEADS], s_new, conv_new)
    return x3_new.reshape(b, t, d), entries


def _lane_block(vals, lane):
    return jnp.pad(vals.astype(F32), ((0, 0), (lane, LANES - lane - vals.shape[-1])))


def kernel(x_prompt, x_sample, cache_mla_ckv, cache_mla_kpe, cache_fox_k, cache_fox_v, cache_fox_logf, state_gdn, state_gdn_conv, cache_mem_k, cache_mem_v, mem_prompt, ln_in_g, ln_in_b, w_in, qa_g, kva_g, w_uq, w_ukv, gdn_conv_w, gdn_a_log, gdn_dt_bias, gdn_norm_g, fox_bf, w_out, w_xq, w_mk, w_mv, w_xo, w_ff1, w_ff2, ln_g, ln_b):
    depth = w_in.shape[0]
    alpha = (2 * depth) ** 0.25
    bp, tp, d = x_prompt.shape
    bs, ts, _ = x_sample.shape
    p_len = cache_mla_ckv.shape[2]

    w_in_p = _pad_w_in(w_in)
    w_uq_p = _pad_w_uq(w_uq)
    w_ukv_p = _pad_w_ukv(w_ukv)
    w_mkv = jnp.concatenate([w_mk, w_mv], -1).astype(BF16)
    gate = jnp.stack([_lane_block(gdn_a_log, GA_LANE), _lane_block(gdn_dt_bias, GA_LANE)], 1)
    fox_bias = _lane_block(fox_bf, FF_LANE)[:, None, :]
    bf = lambda a: a.astype(BF16)
    w_out_b, w_xq_b, w_xo_b, w_ff1_b, w_ff2_b = bf(w_out), bf(w_xq), bf(w_xo), bf(w_ff1), bf(w_ff2)

    tabs_p = _rope_tables(tp, 0)
    tabs_s = _rope_tables(ts, p_len)

    xp = _ln_in(x_prompt.reshape(bp * tp, d), ln_in_g, ln_in_b).reshape(bp, tp, d)
    xs = _ln_in(x_sample.reshape(bs * ts, d), ln_in_g, ln_in_b).reshape(bs, ts, d)
    mem2d = mem_prompt.reshape(bp * N_MEM, d)

    p_new, p_mem, s_new = [], [], []
    for l in range(depth):
        lw = dict(w_in=w_in_p[l], qa_g=qa_g[l], kva_g=kva_g[l], w_uq=w_uq_p[l], w_ukv=w_ukv_p[l],
                  conv_w=gdn_conv_w[l], gate=gate[l], gdn_g=gdn_norm_g[l], fox_bias=fox_bias[l],
                  w_out=w_out_b[l], w_xq=w_xq_b[l], w_xo=w_xo_b[l], w_ff1=w_ff1_b[l], w_ff2=w_ff2_b[l],
                  ln_g=ln_g[l], ln_b=ln_b[l])
        mk, mv = _mem_kv(mem2d, w_mkv[l])
        mk3 = mk.reshape(bp, N_MEM, MEM_W)
        mv3 = mv.reshape(bp, N_MEM, MEM_W)
        xp, ent_p = _encoder_layer(xp, None, mk3, mv3, lw, tabs_p, alpha)
        p_new.append(ent_p)
        p_mem.append((mk.reshape(bp, N_MEM, MEM_HEADS, MEM_HD), mv.reshape(bp, N_MEM, MEM_HEADS, MEM_HD)))
        past = (cache_mla_ckv[l], cache_mla_kpe[l], cache_fox_k[l], cache_fox_v[l], cache_fox_logf[l],
                state_gdn[l], state_gdn_conv[l])
        xs, ent_s = _encoder_layer(xs, past, cache_mem_k[l].reshape(bs, N_MEM, MEM_W),
                                   cache_mem_v[l].reshape(bs, N_MEM, MEM_W), lw, tabs_s, alpha)
        s_new.append(ent_s)

    stack = lambda entries, i: jnp.stack([e[i] for e in entries])
    return (xp, xs,
            stack(p_new, 0), stack(p_new, 1), stack(p_new, 2), stack(p_new, 3), stack(p_new, 4),
            stack(p_new, 5), stack(p_new, 6), stack(p_mem, 0), stack(p_mem, 1),
            stack(s_new, 0), stack(s_new, 1), stack(s_new, 2), stack(s_new, 3), stack(s_new, 4),
            stack(s_new, 5), stack(s_new, 6))
```

```python
import functools
import math

import jax
import jax.numpy as jnp
from jax import lax
from jax.experimental import pallas as pl
from jax.experimental.pallas import tpu as pltpu

F32 = jnp.float32
BF16 = jnp.bfloat16

D_MODEL = 1024
CHUNK = 64
MLA_HEADS = 4
MLA_NOPE = 64
MLA_ROPE = 32
MLA_V = 64
MLA_Q_LORA = 384
MLA_KV_LORA = 256
ROPE_THETA = 10000.0
GDN_HEADS = 4
GDN_DK = 128
GDN_DV = 128
CONV_W = 4
FOX_HEADS = 4
FOX_HD = 64
N_MEM = 256
MEM_HEADS = 4
MEM_HD = 128
D_FF = 4 * D_MODEL
LN_EPS = 1e-5
RMS_EPS = 1e-6
L2_EPS = 1e-6

GDN_QK = GDN_HEADS * GDN_DK
GDN_V = GDN_HEADS * GDN_DV
GDN_CONV_DIM = 2 * GDN_QK + GDN_V
FOX_W = FOX_HEADS * FOX_HD
MLA_W = MLA_HEADS * MLA_V
MEM_W = MEM_HEADS * MEM_HD

LANES = 128
SUBLANES = 8
ROW_TILE = 512
ATTN_BLOCK = 512
FLASH_ROW_SPLIT = 2
SCAN_BLOCK = 256
VMEM_LIMIT = 56 * 1024 * 1024
NEG_BIG = -1e30
LOG2E = 1.0 / math.log(2.0)

FF_LANE = 0
GA_LANE = 4
GB_LANE = 8
ROPE_LO = MLA_NOPE
ROPE_HI = MLA_NOPE + MLA_ROPE


def _params(*sem):
    return pltpu.CompilerParams(dimension_semantics=sem, vmem_limit_bytes=VMEM_LIMIT)


def _resident(shape):
    nd = len(shape)
    return pl.BlockSpec(shape, lambda *_: (0,) * nd)


def _layer_norm(y, g, b):
    mu = jnp.mean(y, axis=-1, keepdims=True)
    yc = y - mu
    var = jnp.mean(yc * yc, axis=-1, keepdims=True)
    return yc * lax.rsqrt(var + LN_EPS) * g + b


def _sigmoid(x):
    return 1.0 / (1.0 + jnp.exp(-x))


def _softplus(x):
    return jnp.maximum(x, 0.0) + jnp.log1p(jnp.exp(-jnp.abs(x)))


def _dot(a, b):
    return jnp.dot(a, b, preferred_element_type=F32)


def _dot_nt(a, b):
    return lax.dot_general(a, b, (((1,), (1,)), ((), ())), preferred_element_type=F32)


def _split3(x):
    hi = x.astype(BF16)
    r1 = x - hi.astype(F32)
    mid = r1.astype(BF16)
    lo = (r1 - mid.astype(F32)).astype(BF16)
    return hi, mid, lo


def _cumsum_rows(tri, x):
    hi, mid, lo = _split3(x)
    return _dot(tri, hi) + _dot(tri, mid) + _dot(tri, lo)


def _tri(n):
    r = lax.broadcasted_iota(jnp.int32, (n, n), 0)
    c = lax.broadcasted_iota(jnp.int32, (n, n), 1)
    return (r >= c).astype(BF16)


def _rope_table_kernel(cos_ref, sin_ref, *, start):
    shape = cos_ref.shape
    lane = lax.broadcasted_iota(jnp.int32, shape, 1)
    row = lax.broadcasted_iota(jnp.int32, shape, 0)
    half = MLA_ROPE // 2
    idx = ((lane - ROPE_LO) & (half - 1)).astype(F32)
    inv = jnp.exp(idx * (-(2.0 / MLA_ROPE) * math.log(ROPE_THETA)))
    ang = (row + start).astype(F32) * inv
    in_rope = (lane >= ROPE_LO) & (lane < ROPE_HI)
    cos_ref[...] = jnp.where(in_rope, jnp.cos(ang), 1.0)
    sin_ref[...] = jnp.where(in_rope, jnp.sin(ang), 0.0)


def _rope_tables(t, start):
    shp = jax.ShapeDtypeStruct((t, LANES), F32)
    return pl.pallas_call(functools.partial(_rope_table_kernel, start=start),
                          out_shape=(shp, shp), name="rope_tables")()


def _ln_kernel(x_ref, g_ref, b_ref, o_ref):
    o_ref[...] = _layer_norm(x_ref[...], g_ref[...], b_ref[...])


def _ln_in(x2d, g, b):
    m, d = x2d.shape
    tm = min(ROW_TILE, m)
    return pl.pallas_call(
        _ln_kernel, grid=(m // tm,),
        in_specs=[pl.BlockSpec((tm, d), lambda i: (i, 0)), _resident((1, d)), _resident((1, d))],
        out_specs=pl.BlockSpec((tm, d), lambda i: (i, 0)),
        out_shape=jax.ShapeDtypeStruct((m, d), F32),
        compiler_params=_params("parallel"), name="ln_in")(x2d, g.reshape(1, d), b.reshape(1, d))


IN_MLA_W = MLA_Q_LORA + MLA_KV_LORA + 2 * LANES
IN_GDN_W = GDN_CONV_DIM + GDN_V
IN_PAD_COLS = (IN_MLA_W, IN_GDN_W, LANES, FOX_W, FOX_W, FOX_W)
IN_PAD_OFFS = tuple(sum(IN_PAD_COLS[:n]) for n in range(len(IN_PAD_COLS) + 1))


def _rms(x, g):
    return x * lax.rsqrt(jnp.mean(x * x, axis=-1, keepdims=True) + RMS_EPS) * g


def _mla_keys_values(ckv, kpe, w_ref, k_ref, v_ref):
    kv = _dot(ckv.astype(BF16), w_ref[...])
    kw = MLA_HEADS * LANES
    for h in range(MLA_HEADS):
        k_ref[:, h * LANES:(h + 1) * LANES] = (kv[:, h * LANES:(h + 1) * LANES] + kpe).astype(BF16)
    v_ref[...] = kv[:, kw:].astype(BF16)


def _in_proj_kernel(x_ref, w_ref, cos_ref, sin_ref, qg_ref, kg_ref, wq_ref, *rest, with_kv, n_alias):
    if with_kv:
        wkv_ref, rest = rest[0], rest[1:]
    rest = rest[n_alias:]
    if with_kv:
        q_ref, ckv_ref, kpe_ref, k_ref, v_ref = rest[:5]
    else:
        q_ref, ckv_ref, kpe_ref = rest[:3]
    o_refs = rest[-(len(IN_PAD_COLS) - 1):]
    xb = x_ref[...].astype(BF16)
    p = _dot(xb, w_ref[:, :IN_MLA_W])
    cos = cos_ref[...]
    sin = sin_ref[...]
    c_q = _rms(p[:, :MLA_Q_LORA], qg_ref[...]).astype(BF16)
    qq = _dot(c_q, wq_ref[...])
    hw = MLA_HEADS * LANES
    for h in range(MLA_HEADS):
        a = qq[:, h * LANES:(h + 1) * LANES]
        b = qq[:, hw + h * LANES:hw + (h + 1) * LANES]
        q_ref[:, h * LANES:(h + 1) * LANES] = (a * cos + b * sin).astype(BF16)
    lo = MLA_Q_LORA
    ckv = _rms(p[:, lo:lo + MLA_KV_LORA], kg_ref[...])
    lo += MLA_KV_LORA
    kpe = p[:, lo:lo + LANES] * cos + p[:, lo + LANES:lo + 2 * LANES] * sin
    ckv_ref[...] = ckv
    kpe_ref[...] = kpe
    if with_kv:
        _mla_keys_values(ckv, kpe, wkv_ref, k_ref, v_ref)
    for o_ref, lo, hi in zip(o_refs, IN_PAD_OFFS[1:-1], IN_PAD_OFFS[2:]):
        o_ref[...] = _dot(xb, w_ref[:, lo:hi])


def _in_proj(x2d, w_pad, cos, sin, qa_g, kva_g, w_uq2, w_ukv2, t, stack=None):
    m, d = x2d.shape
    tm = min(ROW_TILE, m)
    if t < tm:
        cos, sin = jnp.tile(cos, (tm // t, 1)), jnp.tile(sin, (tm // t, 1))
    nt = max(t // tm, 1)
    with_kv = w_ukv2 is not None
    row = lambda w: pl.BlockSpec((tm, w), lambda i: (i, 0))
    tab = pl.BlockSpec((tm, LANES), lambda i: (i % nt, 0))
    in_specs = [row(d), _resident(w_pad.shape), tab, tab, _resident((1, MLA_Q_LORA)), _resident((1, MLA_KV_LORA)),
                _resident(w_uq2.shape)]
    args = [x2d, w_pad, cos, sin, qa_g.reshape(1, -1), kva_g.reshape(1, -1), w_uq2]
    outs = [(MLA_HEADS * LANES, BF16, False), (MLA_KV_LORA, F32, True), (LANES, F32, False)]
    if with_kv:
        in_specs.append(_resident(w_ukv2.shape))
        args.append(w_ukv2)
        outs += [(MLA_HEADS * LANES, BF16, False), (MLA_W, BF16, False)]
    outs += [(w, F32, False) for w in IN_PAD_COLS[1:-2]] + [(FOX_W, F32, True), (FOX_W, F32, True)]
    if stack is None:
        outs = [(w, dt, False) for w, dt, _ in outs]
        layer = depth = bufs = None
    else:
        layer, depth, bufs = stack
    slab = lambda w: pl.BlockSpec((None, tm, w), lambda i: (layer, i, 0))
    aliases = {}
    if bufs is not None:
        slab_outs = [n for n, (_, _, stacked) in enumerate(outs) if stacked]
        aliases = {len(args) + k: n for k, n in enumerate(slab_outs)}
        in_specs += [pl.BlockSpec(memory_space=pl.ANY)] * len(bufs)
        args += list(bufs)
    return pl.pallas_call(
        functools.partial(_in_proj_kernel, with_kv=with_kv, n_alias=len(aliases)), grid=(m // tm,),
        in_specs=in_specs,
        out_specs=[slab(w) if stacked else row(w) for w, _, stacked in outs],
        out_shape=[jax.ShapeDtypeStruct((depth, m, w) if stacked else (m, w), dt) for w, dt, stacked in outs],
        input_output_aliases=aliases,
        compiler_params=_params("parallel"), name="in_proj")(*args)


def _pad_w_in(w_in):
    offs = [0]
    for s in (MLA_Q_LORA, MLA_KV_LORA, MLA_ROPE, GDN_QK, GDN_QK, GDN_V, GDN_V, GDN_HEADS, GDN_HEADS,
              FOX_W, FOX_W, FOX_W, FOX_HEADS):
        offs.append(offs[-1] + s)
    cq, ckv, kpe, gq, gk, gv, gz, ga, gb, fq, fk, fv, ff = [w_in[..., a:b] for a, b in zip(offs[:-1], offs[1:])]
    half = MLA_ROPE // 2
    kpe_rot = jnp.concatenate([-kpe[..., half:], kpe[..., :half]], -1)

    def z(n):
        return jnp.zeros(w_in.shape[:-1] + (n,), w_in.dtype)

    def rope_block(w):
        return jnp.concatenate([z(ROPE_LO), w, z(LANES - ROPE_HI)], -1)

    small = jnp.concatenate([ff, ga, gb, z(LANES - 3 * GDN_HEADS)], -1)
    return jnp.concatenate([cq, ckv, rope_block(kpe), rope_block(kpe_rot), gq, gk, gv, gz, small, fq, fk, fv],
                           -1).astype(BF16)


def _pad_w_uq(w_uq):
    half = MLA_ROPE // 2
    w = w_uq.reshape(w_uq.shape[:-1] + (MLA_HEADS, MLA_NOPE + MLA_ROPE))
    nope, rp = w[..., :MLA_NOPE], w[..., MLA_NOPE:]
    rot = jnp.concatenate([-rp[..., half:], rp[..., :half]], -1)
    zpad = jnp.zeros(w.shape[:-1] + (LANES - ROPE_HI,), w.dtype)
    plain = jnp.concatenate([nope, rp, zpad], -1)
    rotated = jnp.concatenate([jnp.zeros_like(nope), rot, zpad], -1)
    flat = lambda a: a.reshape(a.shape[:-2] + (MLA_HEADS * LANES,))
    return jnp.concatenate([flat(plain), flat(rotated)], -1).astype(BF16)


def _mla_kv_kernel(ckv_ref, kpe_ref, w_ref, k_ref, v_ref):
    _mla_keys_values(ckv_ref[...], kpe_ref[...], w_ref, k_ref, v_ref)


def _mla_kv(ckv_all, kpe_all, w_ukv2):
    r = ckv_all.shape[0]
    tm = min(ROW_TILE, r)
    row = lambda w: pl.BlockSpec((tm, w), lambda i: (i, 0))
    return pl.pallas_call(
        _mla_kv_kernel, grid=(r // tm,),
        in_specs=[row(MLA_KV_LORA), row(LANES), _resident(w_ukv2.shape)],
        out_specs=[row(MLA_HEADS * LANES), row(MLA_W)],
        out_shape=[jax.ShapeDtypeStruct((r, MLA_HEADS * LANES), BF16), jax.ShapeDtypeStruct((r, MLA_W), BF16)],
        compiler_params=_params("parallel"), name="mla_kv")(ckv_all, kpe_all, w_ukv2)


def _pad_w_ukv(w_ukv):
    w = w_ukv.reshape(w_ukv.shape[:-1] + (MLA_HEADS, MLA_NOPE + MLA_V))
    nope, v = w[..., :MLA_NOPE], w[..., MLA_NOPE:]
    kpad = jnp.concatenate([nope, jnp.zeros(nope.shape[:-1] + (LANES - MLA_NOPE,), w.dtype)], -1)
    flat = lambda a: a.reshape(a.shape[:-2] + (-1,))
    return jnp.concatenate([flat(kpad), flat(v)], -1).astype(BF16)


def _flash_kernel(*refs, tq, tk, q_start, kv_len, kv_pad, causal, scale, packed_qk, has_f, heads):
    if has_f:
        q_ref, k_ref, v_ref, fq_ref, fk_ref, o_ref, m_sc, acc_sc, fq_sc = refs
    else:
        q_ref, k_ref, v_ref, o_ref, m_sc, acc_sc = refs
    i = pl.program_id(1)
    j = pl.program_id(2)
    nk = pl.num_programs(2)

    @pl.when(j == 0)
    def _():
        m_sc[...] = jnp.full(m_sc.shape, NEG_BIG, F32)
        acc_sc[...] = jnp.zeros(acc_sc.shape, F32)
        if has_f:
            for h in range(heads):
                fq_sc[h] = jnp.broadcast_to(fq_ref[0, :, FF_LANE + h:FF_LANE + h + 1], (tq, LANES))

    qmin = q_start + i * tq
    qmax = qmin + tq - 1
    kmin = j * tk
    kmax = kmin + tk - 1
    if causal == "frame":
        any_vis = kmin <= qmax
        all_vis = kmax <= qmin
    else:
        any_vis = (kmin // CHUNK) <= (qmax // CHUNK)
        all_vis = (kmax // CHUNK) <= (qmin // CHUNK)
    if kv_len < kv_pad:
        all_vis = jnp.logical_and(all_vis, kmax < kv_len)

    fold = math.log2(scale) == round(math.log2(scale))
    half_lane = lax.broadcasted_iota(jnp.int32, (1, LANES), 1) < (LANES // 2)

    def body(masked):
        if masked:
            qpos = qmin + lax.broadcasted_iota(jnp.int32, (tq, tk), 0)
            kpos = kmin + lax.broadcasted_iota(jnp.int32, (tq, tk), 1)
            if causal == "frame":
                vis = kpos <= qpos
            else:
                vis = (kpos // CHUNK) <= (qpos // CHUNK)
            if kv_len < kv_pad:
                vis = jnp.logical_and(vis, kpos < kv_len)
        nsplit = FLASH_ROW_SPLIT if tq % (FLASH_ROW_SPLIT * LANES) == 0 else 1
        rs = tq // nsplit
        units = [(h, r) for h in range(heads) for r in range(nsplit)]
        c = LOG2E if fold else LOG2E * scale
        k_cache, v_cache = {}, {}

        def k_block(b0):
            if b0 not in k_cache:
                k_cache[b0] = k_ref[0, :, b0 * LANES:(b0 + 1) * LANES].astype(BF16)
            return k_cache[b0]

        def v_block(h):
            if h not in v_cache:
                vpair = v_ref[0, :, (h // 2) * LANES:(h // 2 + 1) * LANES].astype(BF16)
                mine = half_lane if h % 2 == 0 else jnp.logical_not(half_lane)
                v_cache[h] = jnp.where(mine, vpair, jnp.ones_like(vpair))
            return v_cache[h]

        def scores(h, r):
            rows = slice(r * rs, (r + 1) * rs)
            b0 = h // 2 if packed_qk else h
            qh = q_ref[0, rows, b0 * LANES:(b0 + 1) * LANES]
            if packed_qk:
                qh = jnp.where(half_lane if h % 2 == 0 else jnp.logical_not(half_lane), qh, 0.0)
            if fold:
                qh = qh * scale
            return _dot_nt(qh.astype(BF16), k_block(b0))

        def softmax_part(h, r, s):
            rows = slice(r * rs, (r + 1) * rs)
            if has_f:
                s = s - fk_ref[0, FF_LANE + h:FF_LANE + h + 1, :]
            if masked:
                s = jnp.where(vis[rows], s, NEG_BIG)
            m_old = m_sc[h, rows]
            row_max = jnp.max(s, axis=-1, keepdims=True)
            if has_f:
                fq_b = fq_sc[h, rows]
                m_new = jnp.maximum(m_old, row_max + fq_b)
                shift = m_new - fq_b
            else:
                m_new = jnp.maximum(m_old, row_max)
                shift = m_new
            m_sc[h, rows] = m_new
            alpha = jnp.exp2((m_old - m_new) * c)
            e = jnp.concatenate([jnp.exp2((s[:, b * LANES:(b + 1) * LANES] - shift) * c)
                                 for b in range(tk // LANES)], -1)
            return alpha, e.astype(BF16)

        def weighted_values(h, r, alpha, e):
            rows = slice(r * rs, (r + 1) * rs)
            acc_sc[h, rows] = acc_sc[h, rows] * alpha + _dot(e, v_block(h))

        s_q, e_q = {}, {}
        for n in range(len(units) + 2):
            if n < len(units):
                s_q[n] = scores(*units[n])
            if 0 <= n - 1 < len(units):
                e_q[n - 1] = softmax_part(*units[n - 1], s_q.pop(n - 1))
            if 0 <= n - 2 < len(units):
                weighted_values(*units[n - 2], *e_q.pop(n - 2))

    @pl.when(all_vis)
    def _():
        body(False)

    @pl.when(jnp.logical_and(any_vis, jnp.logical_not(all_vis)))
    def _():
        body(True)

    @pl.when(j == nk - 1)
    def _():
        for pair in range(heads // 2):
            a_lo = acc_sc[2 * pair]
            a_hi = acc_sc[2 * pair + 1]
            lo = a_lo / pltpu.roll(a_lo, LANES // 2, 1)
            hi = a_hi / pltpu.roll(a_hi, LANES // 2, 1)
            o_ref[0, :, pair * LANES:(pair + 1) * LANES] = jnp.where(half_lane, lo, hi).astype(o_ref.dtype)


def _flash(q, k, v, *, causal, scale, packed_qk, kv_len, fq=None, fk=None, tq=None, tk=None, kv_row0=0):
    b, t, wq = q.shape
    lp = k.shape[1]
    heads = v.shape[2] // (LANES // 2)
    tq = tq or min(ATTN_BLOCK, t)
    tk = tk or (min(ATTN_BLOCK, lp) if t >= ATTN_BLOCK else lp)
    q_start = kv_len - t
    nq, nk = t // tq, lp // tk

    def last_block(i):
        qmax = q_start + i * tq + tq - 1
        last = qmax if causal == "frame" else (qmax // CHUNK) * CHUNK + CHUNK - 1
        return jnp.minimum(last // tk, nk - 1)

    kv_map = lambda bi, i, j: (kv_row0 + bi, jnp.minimum(j, last_block(i)), 0)
    in_specs = [pl.BlockSpec((1, tq, wq), lambda bi, i, j: (bi, i, 0)),
                pl.BlockSpec((1, tk, wq), kv_map),
                pl.BlockSpec((1, tk, v.shape[2]), kv_map)]
    args = [q, k, v]
    has_f = fq is not None
    if has_f:
        in_specs += [pl.BlockSpec((1, tq, LANES), lambda bi, i, j: (bi, i + q_start // tq, 0)),
                     pl.BlockSpec((1, SUBLANES, tk), lambda bi, i, j: (bi, 0, jnp.minimum(j, last_block(i))))]
        args += [fq, fk]
        assert q_start % tq == 0
    kern = functools.partial(_flash_kernel, tq=tq, tk=tk, q_start=q_start, kv_len=kv_len, kv_pad=lp,
                             causal=causal, scale=scale, packed_qk=packed_qk, has_f=has_f, heads=heads)
    return pl.pallas_call(
        kern, grid=(b, nq, nk), in_specs=in_specs,
        out_specs=pl.BlockSpec((1, tq, v.shape[2]), lambda bi, i, j: (bi, i, 0)),
        out_shape=jax.ShapeDtypeStruct((b, t, v.shape[2]), BF16),
        scratch_shapes=[pltpu.VMEM((heads, tq, LANES), F32)] * (3 if has_f else 2),
        compiler_params=_params("parallel", "parallel", "arbitrary"),
        name="flash_" + causal)(*args)


def _fox_scan_kernel(*refs, p_len, t_len, l_pad):
    if p_len:
        small_ref, bias_ref, past_ref, logf_ref, fc_ref, fct_ref = refs
    else:
        small_ref, bias_ref, logf_ref, fc_ref, fct_ref = refs
    x = small_ref[0] + bias_ref[...]
    logf = jnp.minimum(x, 0.0) - jnp.log1p(jnp.exp(-jnp.abs(x)))
    logf_ref[0] = logf
    bs = SCAN_BLOCK
    tri = _tri(bs)
    carry = jnp.zeros((1, LANES), F32)
    for blk in range(l_pad // bs):
        lo = blk * bs
        if lo + bs <= p_len:
            xb = past_ref[0, lo:lo + bs, :]
        elif lo >= p_len and lo + bs <= p_len + t_len:
            xb = logf[lo - p_len:lo - p_len + bs]
        else:
            assert lo == p_len and t_len < bs
            xb = jnp.concatenate([logf, jnp.zeros((bs - t_len, LANES), F32)], 0)
        cs = _cumsum_rows(tri, xb) + carry
        carry = cs[bs - 1:bs, :]
        fc_ref[0, lo:lo + bs, :] = cs
        fct_ref[0, :, lo:lo + bs] = cs.T[:SUBLANES, :]


def _fox_scan(small3, bias128, past128, l_pad):
    b, t, _ = small3.shape
    p_len = 0 if past128 is None else past128.shape[1]
    per_b = lambda r, w: pl.BlockSpec((1, r, w), lambda bi: (bi, 0, 0))
    in_specs = [per_b(t, LANES), _resident((1, LANES))]
    args = [small3, bias128]
    if p_len:
        in_specs.append(per_b(p_len, LANES))
        args.append(past128)
    return pl.pallas_call(
        functools.partial(_fox_scan_kernel, p_len=p_len, t_len=t, l_pad=l_pad), grid=(b,),
        in_specs=in_specs,
        out_specs=[per_b(t, LANES), per_b(l_pad, LANES), per_b(SUBLANES, l_pad)],
        out_shape=[jax.ShapeDtypeStruct((b, t, LANES), F32), jax.ShapeDtypeStruct((b, l_pad, LANES), F32),
                   jax.ShapeDtypeStruct((b, SUBLANES, l_pad), F32)],
        compiler_params=_params("parallel"), name="fox_scan")(*args)


GDN_BLOCK = 256


def _gdn_kernel(qkv_ref, z_ref, small_ref, convw_ref, gate_ref, ng_ref, conv0_ref, s0_ref,
                o_ref, sout_ref, prev_sc, s_sc, *, blk, c_len):
    step = pl.program_id(1)
    nsteps = pl.num_programs(1)
    nsub = blk // c_len

    @pl.when(step == 0)
    def _():
        prev_sc[...] = conv0_ref[0]
        s_sc[...] = s0_ref[0]

    def causal_conv(rows_in):
        out = rows_in * convw_ref[CONV_W - 1:CONV_W, :]
        for s in range(1, CONV_W):
            out = out + pltpu.roll(rows_in, s, 0) * convw_ref[CONV_W - 1 - s:CONV_W - s, :]
        return out

    x = qkv_ref[0]
    first = causal_conv(jnp.concatenate([prev_sc[...], x[:SUBLANES]], 0))[SUBLANES:]
    y = jnp.concatenate([first, causal_conv(x)[SUBLANES:]], 0)
    prev_sc[...] = x[blk - SUBLANES:]
    y = y * _sigmoid(y)

    ri = lax.broadcasted_iota(jnp.int32, (blk, blk), 0)
    ci = lax.broadcasted_iota(jnp.int32, (blk, blk), 1)
    same = (ri // c_len) == (ci // c_len)
    incl = jnp.logical_and(ri >= ci, same)
    strict = jnp.logical_and(ri > ci, same)

    small = small_ref[0]
    g128 = -jnp.exp(gate_ref[0:1, :]) * _softplus(small + gate_ref[1:2, :])
    beta128 = _sigmoid(small)
    gcum = _cumsum_rows(incl.astype(BF16), g128)
    gcum_t = gcum.T
    eg128 = jnp.exp(gcum)
    glast128 = jnp.concatenate(
        [jnp.broadcast_to(gcum[(c + 1) * c_len - 1:(c + 1) * c_len, :], (c_len, LANES)) for c in range(nsub)], 0)
    ekd128 = jnp.exp(glast128 - gcum)
    nlev = int(math.log2(c_len))
    assert 2 ** nlev == c_len

    heads = range(GDN_HEADS)
    kh_l, kb_l, a_l, decay_l, rhs_l, qcb_l, qd_l, kdt_l = [], [], [], [], [], [], [], []
    for h in heads:
        qh = y[:, h * GDN_DK:(h + 1) * GDN_DK]
        kh = y[:, GDN_QK + h * GDN_DK:GDN_QK + (h + 1) * GDN_DK]
        vh = y[:, 2 * GDN_QK + h * GDN_DV:2 * GDN_QK + (h + 1) * GDN_DV]
        qh = qh * lax.rsqrt(jnp.sum(qh * qh, axis=-1, keepdims=True) + L2_EPS)
        kh = kh * lax.rsqrt(jnp.sum(kh * kh, axis=-1, keepdims=True) + L2_EPS)
        lane = GA_LANE + h
        g_col = gcum[:, lane:lane + 1]
        g_row = gcum_t[lane:lane + 1, :]
        eg_col = eg128[:, lane:lane + 1]
        beta = beta128[:, GB_LANE + h:GB_LANE + h + 1]
        decay = jnp.exp(jnp.where(incl, g_col - g_row, NEG_BIG))
        kb = kh.astype(BF16)
        a_l.append(jnp.where(strict, beta * _dot_nt(kb, kb) * decay, 0.0))
        rhs_l.append(jnp.concatenate([vh * beta, kh * (beta * eg_col)], -1))
        qc = qh * (GDN_DK ** -0.5)
        qcb_l.append(qc.astype(BF16))
        qd_l.append((qc * eg_col).astype(BF16))
        kdt_l.append((kh * ekd128[:, lane:lane + 1]).T.astype(BF16))
        kb_l.append(kb)
        decay_l.append(decay)

    tm_l = [-a for a in a_l]
    pb_l = [a.astype(BF16) for a in a_l]
    for _ in range(1, nlev):
        pw_l = [_dot(pb, pb) for pb in pb_l]
        pb_l = [pw.astype(BF16) for pw in pw_l]
        tm_l = [tm + pw + _dot(tm.astype(BF16), pb) for tm, pw, pb in zip(tm_l, pw_l, pb_l)]
    sol_l = [rhs + _dot(tm.astype(BF16), rhs.astype(BF16)) for tm, rhs in zip(tm_l, rhs_l)]
    u_l = [sol[:, :GDN_DV] for sol in sol_l]
    wb_l = [sol[:, GDN_DV:].astype(BF16) for sol in sol_l]
    qk_l = [(_dot_nt(qcb, kb) * decay).astype(BF16) for qcb, kb, decay in zip(qcb_l, kb_l, decay_l)]

    s_l = [s_sc[h] for h in heads]
    v_parts = [[] for _ in heads]
    qs_parts = [[] for _ in heads]
    for c in range(nsub):
        rows = slice(c * c_len, (c + 1) * c_len)
        r_l = [_dot(jnp.concatenate([wb_l[h][rows], qd_l[h][rows]], 0), s_l[h].astype(BF16)) for h in heads]
        for h in heads:
            v_parts[h].append(u_l[h][rows] - r_l[h][:c_len])
            qs_parts[h].append(r_l[h][c_len:])
        s_l = [s_l[h] * eg128[(c + 1) * c_len - 1:(c + 1) * c_len, GA_LANE + h:GA_LANE + h + 1]
               + _dot(kdt_l[h][:, rows], v_parts[h][c].astype(BF16)) for h in heads]
    for h in heads:
        s_sc[h] = s_l[h]
        v_all = v_parts[h][0] if nsub == 1 else jnp.concatenate(v_parts[h], 0)
        qs_all = qs_parts[h][0] if nsub == 1 else jnp.concatenate(qs_parts[h], 0)
        o = qs_all + _dot(qk_l[h], v_all.astype(BF16))
        zh = z_ref[0, :, h * GDN_DV:(h + 1) * GDN_DV]
        o = _rms(o, ng_ref[...]) * (zh * _sigmoid(zh))
        o_ref[0, :, h * GDN_DV:(h + 1) * GDN_DV] = o.astype(o_ref.dtype)

    @pl.when(step == nsteps - 1)
    def _():
        sout_ref[0] = s_sc[...]


def _gdn(p_gdn3, small3, conv_w, gate2, norm_g, conv0, s0, c_len):
    b, t, _ = p_gdn3.shape
    blk = min(GDN_BLOCK, t)
    return pl.pallas_call(
        functools.partial(_gdn_kernel, blk=blk, c_len=c_len), grid=(b, t // blk),
        in_specs=[pl.BlockSpec((1, blk, GDN_CONV_DIM), lambda bi, c: (bi, c, 0)),
                  pl.BlockSpec((1, blk, GDN_V), lambda bi, c: (bi, c, GDN_CONV_DIM // GDN_V)),
                  pl.BlockSpec((1, blk, LANES), lambda bi, c: (bi, c, 0)),
                  _resident((CONV_W, GDN_CONV_DIM)), _resident((2, LANES)), _resident((1, GDN_DV)),
                  pl.BlockSpec((1, SUBLANES, GDN_CONV_DIM), lambda bi, c: (bi, 0, 0)),
                  pl.BlockSpec((1, GDN_HEADS, GDN_DK, GDN_DV), lambda bi, c: (bi, 0, 0, 0))],
        out_specs=[pl.BlockSpec((1, blk, GDN_V), lambda bi, c: (bi, c, 0)),
                   pl.BlockSpec((1, GDN_HEADS, GDN_DK, GDN_DV), lambda bi, c: (bi, 0, 0, 0))],
        out_shape=[jax.ShapeDtypeStruct((b, t, GDN_V), BF16),
                   jax.ShapeDtypeStruct((b, GDN_HEADS, GDN_DK, GDN_DV), F32)],
        scratch_shapes=[pltpu.VMEM((SUBLANES, GDN_CONV_DIM), F32), pltpu.VMEM((GDN_HEADS, GDN_DK, GDN_DV), F32)],
        compiler_params=_params("parallel", "arbitrary"), name="gdn")(
            p_gdn3, p_gdn3, small3, conv_w, gate2, norm_g.reshape(1, -1), conv0, s0)


def _out_ln_kernel(oa_ref, ob_ref, oc_ref, x_ref, w_ref, g_ref, b_ref, wq_ref, x1_ref, qm_ref, *, alpha):
    mix = (_dot(oa_ref[...], w_ref[:MLA_W, :]) + _dot(ob_ref[...], w_ref[MLA_W:MLA_W + GDN_V, :])
           + _dot(oc_ref[...], w_ref[MLA_W + GDN_V:, :]))
    x1 = _layer_norm(alpha * x_ref[...] + mix, g_ref[...], b_ref[...])
    x1_ref[...] = x1
    qm_ref[...] = _dot(x1.astype(BF16), wq_ref[...]).astype(BF16)


def _out_ln(o_a, o_b, o_c, x2d, w_out, g, b, w_xq, alpha):
    m, d = x2d.shape
    tm = min(ROW_TILE, m)
    row = lambda w: pl.BlockSpec((tm, w), lambda i: (i, 0))
    return pl.pallas_call(
        functools.partial(_out_ln_kernel, alpha=alpha), grid=(m // tm,),
        in_specs=[row(MLA_W), row(GDN_V), row(FOX_W), row(d), _resident(w_out.shape), _resident((1, d)),
                  _resident((1, d)), _resident(w_xq.shape)],
        out_specs=[row(d), row(MEM_W)],
        out_shape=[jax.ShapeDtypeStruct((m, d), F32), jax.ShapeDtypeStruct((m, MEM_W), BF16)],
        compiler_params=_params("parallel"), name="out_ln")(
            o_a, o_b, o_c, x2d, w_out, g.reshape(1, d), b.reshape(1, d), w_xq)


def _mem_kv_kernel(m_ref, w_ref, k_ref, v_ref):
    kv = _dot(m_ref[...].astype(BF16), w_ref[...])
    k_ref[...] = kv[:, :MEM_W]
    v_ref[...] = kv[:, MEM_W:]


def _mem_kv(mem2d, w_mkv):
    m, d = mem2d.shape
    depth = w_mkv.shape[0]
    tm = min(ROW_TILE, m)
    slab = pl.BlockSpec((None, tm, MEM_W), lambda l, i: (l, i, 0))
    shp = jax.ShapeDtypeStruct((depth, m, MEM_W), F32)
    return pl.pallas_call(
        _mem_kv_kernel, grid=(depth, m // tm),
        in_specs=[pl.BlockSpec((tm, d), lambda l, i: (i, 0)),
                  pl.BlockSpec((None, d, 2 * MEM_W), lambda l, i: (l, 0, 0))],
        out_specs=[slab, slab], out_shape=[shp, shp],
        compiler_params=_params("parallel", "parallel"), name="mem_kv")(mem2d, w_mkv)


def _mem_attn_kernel(q_ref, k_ref, v_ref, x_ref, w_ref, g_ref, b_ref, o_ref, *, alpha):
    c = (MEM_HD ** -0.5) * LOG2E
    outs = []
    for h in range(MEM_HEADS):
        cols = slice(h * MEM_HD, (h + 1) * MEM_HD)
        s = _dot_nt(q_ref[0, :, cols], k_ref[0, :, cols].astype(BF16))
        e = jnp.exp2((s - jnp.max(s, axis=-1, keepdims=True)) * c)
        pv = _dot(e.astype(BF16), v_ref[0, :, cols].astype(BF16))
        outs.append(pv / jnp.sum(e, axis=-1, keepdims=True))
    o = jnp.concatenate(outs, -1).astype(BF16)
    y = alpha * x_ref[0] + _dot(o, w_ref[...])
    o_ref[0] = _layer_norm(y, g_ref[...], b_ref[...])


def _mem_attn(qm3, mk3, mv3, x3, w_xo, g, b, alpha, kv_row0=0):
    bsz, t, d = x3.shape
    tq = min(ROW_TILE, t)
    nm = mk3.shape[1]
    return pl.pallas_call(
        functools.partial(_mem_attn_kernel, alpha=alpha), grid=(bsz, t // tq),
        in_specs=[pl.BlockSpec((1, tq, MEM_W), lambda bi, i: (bi, i, 0)),
                  pl.BlockSpec((1, nm, MEM_W), lambda bi, i: (kv_row0 + bi, 0, 0)),
                  pl.BlockSpec((1, nm, MEM_W), lambda bi, i: (kv_row0 + bi, 0, 0)),
                  pl.BlockSpec((1, tq, d), lambda bi, i: (bi, i, 0)),
                  _resident(w_xo.shape), _resident((1, d)), _resident((1, d))],
        out_specs=pl.BlockSpec((1, tq, d), lambda bi, i: (bi, i, 0)),
        out_shape=jax.ShapeDtypeStruct((bsz, t, d), F32),
        compiler_params=_params("parallel", "parallel"), name="mem_attn")(
            qm3, mk3, mv3, x3, w_xo, g.reshape(1, d), b.reshape(1, d))


FFN_CHUNK = 1024


def _ffn_kernel(x_ref, w1_ref, w2_ref, g_ref, b_ref, o_ref, *, alpha):
    x = x_ref[...]
    xb = x.astype(BF16)
    acc = alpha * x
    for c in range(D_FF // FFN_CHUNK):
        cols = slice(c * FFN_CHUNK, (c + 1) * FFN_CHUNK)
        hdn = jnp.maximum(_dot(xb, w1_ref[:, cols]), 0.0)
        acc = acc + _dot((hdn * hdn).astype(BF16), w2_ref[cols, :])
    o_ref[...] = _layer_norm(acc, g_ref[...], b_ref[...])


def _ffn(x2d, w1, w2, g, b, alpha):
    m, d = x2d.shape
    tm = min(ROW_TILE, m)
    row = pl.BlockSpec((tm, d), lambda i: (i, 0))
    return pl.pallas_call(
        functools.partial(_ffn_kernel, alpha=alpha), grid=(m // tm,),
        in_specs=[row, _resident(w1.shape), _resident(w2.shape), _resident((1, d)), _resident((1, d))],
        out_specs=row, out_shape=jax.ShapeDtypeStruct((m, d), F32),
        compiler_params=_params("parallel"), name="ffn")(x2d, w1, w2, g.reshape(1, d), b.reshape(1, d))


def _round_up(n, m):
    return -(-n // m) * m


def _encoder_layer(x3, past, mem_k, mem_v, lw, rope_tabs, alpha, stack=None):
    b, t, d = x3.shape
    m = b * t
    x2d = x3.reshape(m, d)
    p_len = 0 if past is None else past[0].shape[1]
    l_len = p_len + t
    l_pad = _round_up(l_len, SCAN_BLOCK)
    cos, sin = rope_tabs

    fused_kv = past is None and l_pad == l_len
    stacked = stack is not None
    assert fused_kv or not stacked
    row0 = stack[0] * b if stacked else 0
    proj = _in_proj(x2d, lw["w_in"], cos, sin, lw["qa_g"], lw["kva_g"], lw["w_uq"],
                    lw["w_ukv"] if fused_kv else None, t, stack)
    q_a, c_kv, kpe128 = proj[:3]
    p_gdn, p_small, p_fq, p_fk, p_fv = proj[-5:]

    c_kv3 = None if stacked else c_kv.reshape(b, t, MLA_KV_LORA)
    kpe3 = kpe128.reshape(b, t, LANES)
    if fused_kv:
        k_a, v_a = proj[3:5]
    else:
        ckv_all, kpe_all = c_kv3, kpe3
        if past is not None:
            kpe_past = jnp.pad(past[1], ((0, 0), (0, 0), (ROPE_LO, LANES - ROPE_HI)))
            ckv_all = jnp.concatenate([past[0], c_kv3], 1)
            kpe_all = jnp.concatenate([kpe_past, kpe3], 1)
        if l_pad > l_len:
            ckv_all = jnp.pad(ckv_all, ((0, 0), (0, l_pad - l_len), (0, 0)))
            kpe_all = jnp.pad(kpe_all, ((0, 0), (0, l_pad - l_len), (0, 0)))
        k_a, v_a = _mla_kv(ckv_all.reshape(b * l_pad, MLA_KV_LORA), kpe_all.reshape(b * l_pad, LANES),
                           lw["w_ukv"])
    o_a = _flash(q_a.reshape(b, t, -1), k_a.reshape(b, l_pad, -1), v_a.reshape(b, l_pad, -1),
                 causal="chunk", scale=(MLA_NOPE + MLA_ROPE) ** -0.5, packed_qk=False, kv_len=l_len)

    small3 = p_small.reshape(b, t, LANES)
    p_gdn3 = p_gdn.reshape(b, t, IN_GDN_W)
    if past is None:
        conv_past = jnp.zeros((b, CONV_W - 1, GDN_CONV_DIM), F32)
        s_past = jnp.zeros((b, GDN_HEADS, GDN_DK, GDN_DV), F32)
    else:
        s_past, conv_past = past[5], past[6]
    conv0 = jnp.pad(conv_past, ((0, 0), (SUBLANES - (CONV_W - 1), 0), (0, 0)))
    o_b, s_new = _gdn(p_gdn3, small3, lw["conv_w"], lw["gate"], lw["gdn_g"], conv0, s_past, min(CHUNK, t))
    conv_new = jnp.concatenate([conv_past, p_gdn3[:, -(CONV_W - 1):, :GDN_CONV_DIM]], 1)[:, -(CONV_W - 1):]

    if past is None:
        logf_past, fk_all, fv_all = None, p_fk.reshape(-1, t, FOX_W), p_fv.reshape(-1, t, FOX_W)
    else:
        logf_past = jnp.pad(past[4], ((0, 0), (0, 0), (FF_LANE, LANES - FF_LANE - FOX_HEADS)))
        fk_all = jnp.concatenate([past[2].reshape(b, p_len, FOX_W), p_fk.reshape(b, t, FOX_W)], 1)
        fv_all = jnp.concatenate([past[3].reshape(b, p_len, FOX_W), p_fv.reshape(b, t, FOX_W)], 1)
    if l_pad > l_len:
        fk_all = jnp.pad(fk_all, ((0, 0), (0, l_pad - l_len), (0, 0)))
        fv_all = jnp.pad(fv_all, ((0, 0), (0, l_pad - l_len), (0, 0)))
    logf128, fcum, fcum_t = _fox_scan(small3, lw["fox_bias"], logf_past, l_pad)
    o_c = _flash(p_fq.reshape(b, t, FOX_W), fk_all, fv_all, causal="frame", scale=FOX_HD ** -0.5,
                 packed_qk=True, kv_len=l_len, fq=fcum, fk=fcum_t, kv_row0=row0)

    x1, qm = _out_ln(o_a.reshape(m, MLA_W), o_b.reshape(m, GDN_V), o_c.reshape(m, FOX_W), x2d, lw["w_out"],
                     lw["ln_g"][0], lw["ln_b"][0], lw["w_xq"], alpha)
    x2 = _mem_attn(qm.reshape(b, t, MEM_W), mem_k, mem_v, x1.reshape(b, t, d), lw["w_xo"],
                   lw["ln_g"][1], lw["ln_b"][1], alpha, kv_row0=row0)
    x3_new = _ffn(x2.reshape(m, d), lw["w_ff1"], lw["w_ff2"], lw["ln_g"][2], lw["ln_b"][2], alpha)

    heads_of = lambda a: None if stacked else a.reshape(b, t, FOX_HEADS, FOX_HD)
    entries = (c_kv3, kpe3[..., ROPE_LO:ROPE_HI], heads_of(p_fk), heads_of(p_fv),
               logf128[..., FF_LANE:FF_LANE + FOX_HEADS], s_new, conv_new)
    return x3_new.reshape(b, t, d), entries, ((c_kv, p_fk, p_fv) if stacked else None)


def _lane_block(vals, lane):
    return jnp.pad(vals.astype(F32), ((0, 0), (lane, LANES - lane - vals.shape[-1])))


def kernel(x_prompt, x_sample, cache_mla_ckv, cache_mla_kpe, cache_fox_k, cache_fox_v, cache_fox_logf, state_gdn, state_gdn_conv, cache_mem_k, cache_mem_v, mem_prompt, ln_in_g, ln_in_b, w_in, qa_g, kva_g, w_uq, w_ukv, gdn_conv_w, gdn_a_log, gdn_dt_bias, gdn_norm_g, fox_bf, w_out, w_xq, w_mk, w_mv, w_xo, w_ff1, w_ff2, ln_g, ln_b):
    depth = w_in.shape[0]
    alpha = (2 * depth) ** 0.25
    bp, tp, d = x_prompt.shape
    bs, ts, _ = x_sample.shape
    p_len = cache_mla_ckv.shape[2]

    w_in_p = _pad_w_in(w_in)
    w_uq_p = _pad_w_uq(w_uq)
    w_ukv_p = _pad_w_ukv(w_ukv)
    w_mkv = jnp.concatenate([w_mk, w_mv], -1).astype(BF16)
    gate = jnp.stack([_lane_block(gdn_a_log, GA_LANE), _lane_block(gdn_dt_bias, GA_LANE)], 1)
    fox_bias = _lane_block(fox_bf, FF_LANE)[:, None, :]
    bf = lambda a: a.astype(BF16)
    w_out_b, w_xq_b, w_xo_b, w_ff1_b, w_ff2_b = bf(w_out), bf(w_xq), bf(w_xo), bf(w_ff1), bf(w_ff2)

    tabs_p = _rope_tables(tp, 0)
    tabs_s = _rope_tables(ts, p_len)

    xp = _ln_in(x_prompt.reshape(bp * tp, d), ln_in_g, ln_in_b).reshape(bp, tp, d)
    xs = _ln_in(x_sample.reshape(bs * ts, d), ln_in_g, ln_in_b).reshape(bs, ts, d)
    mem2d = mem_prompt.reshape(bp * N_MEM, d)

    mk_all, mv_all = _mem_kv(mem2d, w_mkv)
    mk3 = mk_all.reshape(depth * bp, N_MEM, MEM_W)
    mv3 = mv_all.reshape(depth * bp, N_MEM, MEM_W)

    p_new, s_new = [], []
    bufs = None
    for l in range(depth):
        lw = dict(w_in=w_in_p[l], qa_g=qa_g[l], kva_g=kva_g[l], w_uq=w_uq_p[l], w_ukv=w_ukv_p[l],
                  conv_w=gdn_conv_w[l], gate=gate[l], gdn_g=gdn_norm_g[l], fox_bias=fox_bias[l],
                  w_out=w_out_b[l], w_xq=w_xq_b[l], w_xo=w_xo_b[l], w_ff1=w_ff1_b[l], w_ff2=w_ff2_b[l],
                  ln_g=ln_g[l], ln_b=ln_b[l])
        xp, ent_p, bufs = _encoder_layer(xp, None, mk3, mv3, lw, tabs_p, alpha, stack=(l, depth, bufs))
        p_new.append(ent_p)
        past = (cache_mla_ckv[l], cache_mla_kpe[l], cache_fox_k[l], cache_fox_v[l], cache_fox_logf[l],
                state_gdn[l], state_gdn_conv[l])
        xs, ent_s, _ = _encoder_layer(xs, past, cache_mem_k[l].reshape(bs, N_MEM, MEM_W),
                                      cache_mem_v[l].reshape(bs, N_MEM, MEM_W), lw, tabs_s, alpha)
        s_new.append(ent_s)

    ckv_all, fk_all, fv_all = bufs
    stack = lambda entries, i: jnp.stack([e[i] for e in entries])
    mem_shape = (depth, bp, N_MEM, MEM_HEADS, MEM_HD)
    fox_shape = (depth, bp, tp, FOX_HEADS, FOX_HD)
    return (xp, xs,
            ckv_all.reshape(depth, bp, tp, MLA_KV_LORA), stack(p_new, 1), fk_all.reshape(fox_shape),
            fv_all.reshape(fox_shape), stack(p_new, 4), stack(p_new, 5), stack(p_new, 6),
            mk_all.reshape(mem_shape), mv_all.reshape(mem_shape),
            stack(s_new, 0), stack(s_new, 1), stack(s_new, 2), stack(s_new, 3), stack(s_new, 4),
            stack(s_new, 5), stack(s_new, 6))
```

```python
import functools
import math

import jax
import jax.numpy as jnp
from jax import lax
from jax.experimental import pallas as pl
from jax.experimental.pallas import tpu as pltpu

F32 = jnp.float32
BF16 = jnp.bfloat16

D_MODEL = 1024
CHUNK = 64
MLA_HEADS = 4
MLA_NOPE = 64
MLA_ROPE = 32
MLA_V = 64
MLA_Q_LORA = 384
MLA_KV_LORA = 256
ROPE_THETA = 10000.0
GDN_HEADS = 4
GDN_DK = 128
GDN_DV = 128
CONV_W = 4
FOX_HEADS = 4
FOX_HD = 64
N_MEM = 256
MEM_HEADS = 4
MEM_HD = 128
D_FF = 4 * D_MODEL
LN_EPS = 1e-5
RMS_EPS = 1e-6
L2_EPS = 1e-6

GDN_QK = GDN_HEADS * GDN_DK
GDN_V = GDN_HEADS * GDN_DV
GDN_CONV_DIM = 2 * GDN_QK + GDN_V
FOX_W = FOX_HEADS * FOX_HD
MLA_W = MLA_HEADS * MLA_V
MEM_W = MEM_HEADS * MEM_HD

LANES = 128
SUBLANES = 8
ROW_TILE = 512
ATTN_BLOCK = 512
FLASH_ROW_SPLIT = 2
SCAN_BLOCK = 256
VMEM_LIMIT = 56 * 1024 * 1024
NEG_BIG = -1e30
LOG2E = 1.0 / math.log(2.0)

FF_LANE = 0
GA_LANE = 4
GB_LANE = 8
ROPE_LO = MLA_NOPE
ROPE_HI = MLA_NOPE + MLA_ROPE


def _params(*sem):
    return pltpu.CompilerParams(dimension_semantics=sem, vmem_limit_bytes=VMEM_LIMIT)


def _resident(shape):
    nd = len(shape)
    return pl.BlockSpec(shape, lambda *_: (0,) * nd)


def _layer_norm(y, g, b):
    mu = jnp.mean(y, axis=-1, keepdims=True)
    yc = y - mu
    var = jnp.mean(yc * yc, axis=-1, keepdims=True)
    return yc * lax.rsqrt(var + LN_EPS) * g + b


def _sigmoid(x):
    return 1.0 / (1.0 + jnp.exp(-x))


def _softplus(x):
    return jnp.maximum(x, 0.0) + jnp.log1p(jnp.exp(-jnp.abs(x)))


def _dot(a, b):
    return jnp.dot(a, b, preferred_element_type=F32)


def _dot_nt(a, b):
    return lax.dot_general(a, b, (((1,), (1,)), ((), ())), preferred_element_type=F32)


def _split3(x):
    hi = x.astype(BF16)
    r1 = x - hi.astype(F32)
    mid = r1.astype(BF16)
    lo = (r1 - mid.astype(F32)).astype(BF16)
    return hi, mid, lo


def _cumsum_rows(tri, x):
    hi, mid, lo = _split3(x)
    return _dot(tri, hi) + _dot(tri, mid) + _dot(tri, lo)


def _tri(n):
    r = lax.broadcasted_iota(jnp.int32, (n, n), 0)
    c = lax.broadcasted_iota(jnp.int32, (n, n), 1)
    return (r >= c).astype(BF16)


def _rope_table_kernel(cos_ref, sin_ref, *, start):
    shape = cos_ref.shape
    lane = lax.broadcasted_iota(jnp.int32, shape, 1)
    row = lax.broadcasted_iota(jnp.int32, shape, 0)
    half = MLA_ROPE // 2
    idx = ((lane - ROPE_LO) & (half - 1)).astype(F32)
    inv = jnp.exp(idx * (-(2.0 / MLA_ROPE) * math.log(ROPE_THETA)))
    ang = (row + start).astype(F32) * inv
    in_rope = (lane >= ROPE_LO) & (lane < ROPE_HI)
    cos_ref[...] = jnp.where(in_rope, jnp.cos(ang), 1.0)
    sin_ref[...] = jnp.where(in_rope, jnp.sin(ang), 0.0)


def _rope_tables(t, start):
    shp = jax.ShapeDtypeStruct((t, LANES), F32)
    return pl.pallas_call(functools.partial(_rope_table_kernel, start=start),
                          out_shape=(shp, shp), name="rope_tables")()


def _ln_kernel(x_ref, g_ref, b_ref, o_ref):
    o_ref[...] = _layer_norm(x_ref[...], g_ref[...], b_ref[...])


def _ln_in(x2d, g, b):
    m, d = x2d.shape
    tm = min(ROW_TILE, m)
    return pl.pallas_call(
        _ln_kernel, grid=(m // tm,),
        in_specs=[pl.BlockSpec((tm, d), lambda i: (i, 0)), _resident((1, d)), _resident((1, d))],
        out_specs=pl.BlockSpec((tm, d), lambda i: (i, 0)),
        out_shape=jax.ShapeDtypeStruct((m, d), F32),
        compiler_params=_params("parallel"), name="ln_in")(x2d, g.reshape(1, d), b.reshape(1, d))


IN_MLA_W = MLA_Q_LORA + MLA_KV_LORA + 2 * LANES
IN_GDN_W = GDN_CONV_DIM + GDN_V
IN_PAD_COLS = (IN_MLA_W, IN_GDN_W, LANES, FOX_W, FOX_W, FOX_W)
IN_PAD_OFFS = tuple(sum(IN_PAD_COLS[:n]) for n in range(len(IN_PAD_COLS) + 1))


def _rms(x, g):
    return x * lax.rsqrt(jnp.mean(x * x, axis=-1, keepdims=True) + RMS_EPS) * g


def _mla_keys_values(ckv, kpe, w_ref, k_ref, v_ref):
    kv = _dot(ckv.astype(BF16), w_ref[...])
    kw = MLA_HEADS * LANES
    for h in range(MLA_HEADS):
        k_ref[:, h * LANES:(h + 1) * LANES] = (kv[:, h * LANES:(h + 1) * LANES] + kpe).astype(BF16)
    v_ref[...] = kv[:, kw:].astype(BF16)


def _in_proj_kernel(x_ref, w_ref, cos_ref, sin_ref, qg_ref, kg_ref, wq_ref, *rest, with_kv, n_alias):
    if with_kv:
        wkv_ref, rest = rest[0], rest[1:]
    rest = rest[n_alias:]
    if with_kv:
        q_ref, ckv_ref, kpe_ref, k_ref, v_ref = rest[:5]
    else:
        q_ref, ckv_ref, kpe_ref = rest[:3]
    o_refs = rest[-(len(IN_PAD_COLS) - 1):]
    xb = x_ref[...].astype(BF16)
    p = _dot(xb, w_ref[:, :IN_MLA_W])
    cos = cos_ref[...]
    sin = sin_ref[...]
    c_q = _rms(p[:, :MLA_Q_LORA], qg_ref[...]).astype(BF16)
    qq = _dot(c_q, wq_ref[...])
    hw = MLA_HEADS * LANES
    for h in range(MLA_HEADS):
        a = qq[:, h * LANES:(h + 1) * LANES]
        b = qq[:, hw + h * LANES:hw + (h + 1) * LANES]
        q_ref[:, h * LANES:(h + 1) * LANES] = (a * cos + b * sin).astype(BF16)
    lo = MLA_Q_LORA
    ckv = _rms(p[:, lo:lo + MLA_KV_LORA], kg_ref[...])
    lo += MLA_KV_LORA
    kpe = p[:, lo:lo + LANES] * cos + p[:, lo + LANES:lo + 2 * LANES] * sin
    ckv_ref[...] = ckv
    kpe_ref[...] = kpe
    if with_kv:
        _mla_keys_values(ckv, kpe, wkv_ref, k_ref, v_ref)
    for o_ref, lo, hi in zip(o_refs, IN_PAD_OFFS[1:-1], IN_PAD_OFFS[2:]):
        o_ref[...] = _dot(xb, w_ref[:, lo:hi])


def _in_proj(x2d, w_pad, cos, sin, qa_g, kva_g, w_uq2, w_ukv2, t, stack=None):
    m, d = x2d.shape
    tm = min(ROW_TILE, m)
    if t < tm:
        cos, sin = jnp.tile(cos, (tm // t, 1)), jnp.tile(sin, (tm // t, 1))
    nt = max(t // tm, 1)
    with_kv = w_ukv2 is not None
    row = lambda w: pl.BlockSpec((tm, w), lambda i: (i, 0))
    tab = pl.BlockSpec((tm, LANES), lambda i: (i % nt, 0))
    in_specs = [row(d), _resident(w_pad.shape), tab, tab, _resident((1, MLA_Q_LORA)), _resident((1, MLA_KV_LORA)),
                _resident(w_uq2.shape)]
    args = [x2d, w_pad, cos, sin, qa_g.reshape(1, -1), kva_g.reshape(1, -1), w_uq2]
    outs = [(MLA_HEADS * LANES, BF16, False), (MLA_KV_LORA, F32, True), (LANES, F32, False)]
    if with_kv:
        in_specs.append(_resident(w_ukv2.shape))
        args.append(w_ukv2)
        outs += [(MLA_HEADS * LANES, BF16, False), (MLA_W, BF16, False)]
    outs += [(w, F32, False) for w in IN_PAD_COLS[1:-2]] + [(FOX_W, F32, True), (FOX_W, F32, True)]
    if stack is None:
        outs = [(w, dt, False) for w, dt, _ in outs]
        layer = depth = bufs = None
    else:
        layer, depth, bufs = stack
    slab = lambda w: pl.BlockSpec((None, tm, w), lambda i: (layer, i, 0))
    aliases = {}
    if bufs is not None:
        slab_outs = [n for n, (_, _, stacked) in enumerate(outs) if stacked]
        aliases = {len(args) + k: n for k, n in enumerate(slab_outs)}
        in_specs += [pl.BlockSpec(memory_space=pl.ANY)] * len(bufs)
        args += list(bufs)
    return pl.pallas_call(
        functools.partial(_in_proj_kernel, with_kv=with_kv, n_alias=len(aliases)), grid=(m // tm,),
        in_specs=in_specs,
        out_specs=[slab(w) if stacked else row(w) for w, _, stacked in outs],
        out_shape=[jax.ShapeDtypeStruct((depth, m, w) if stacked else (m, w), dt) for w, dt, stacked in outs],
        input_output_aliases=aliases,
        compiler_params=_params("parallel"), name="in_proj")(*args)


def _pad_w_in(w_in):
    offs = [0]
    for s in (MLA_Q_LORA, MLA_KV_LORA, MLA_ROPE, GDN_QK, GDN_QK, GDN_V, GDN_V, GDN_HEADS, GDN_HEADS,
              FOX_W, FOX_W, FOX_W, FOX_HEADS):
        offs.append(offs[-1] + s)
    cq, ckv, kpe, gq, gk, gv, gz, ga, gb, fq, fk, fv, ff = [w_in[..., a:b] for a, b in zip(offs[:-1], offs[1:])]
    half = MLA_ROPE // 2
    kpe_rot = jnp.concatenate([-kpe[..., half:], kpe[..., :half]], -1)

    def z(n):
        return jnp.zeros(w_in.shape[:-1] + (n,), w_in.dtype)

    def rope_block(w):
        return jnp.concatenate([z(ROPE_LO), w, z(LANES - ROPE_HI)], -1)

    small = jnp.concatenate([ff, ga, gb, z(LANES - 3 * GDN_HEADS)], -1)
    return jnp.concatenate([cq, ckv, rope_block(kpe), rope_block(kpe_rot), gq, gk, gv, gz, small, fq, fk, fv],
                           -1).astype(BF16)


def _pad_w_uq(w_uq):
    half = MLA_ROPE // 2
    w = w_uq.reshape(w_uq.shape[:-1] + (MLA_HEADS, MLA_NOPE + MLA_ROPE))
    nope, rp = w[..., :MLA_NOPE], w[..., MLA_NOPE:]
    rot = jnp.concatenate([-rp[..., half:], rp[..., :half]], -1)
    zpad = jnp.zeros(w.shape[:-1] + (LANES - ROPE_HI,), w.dtype)
    plain = jnp.concatenate([nope, rp, zpad], -1)
    rotated = jnp.concatenate([jnp.zeros_like(nope), rot, zpad], -1)
    flat = lambda a: a.reshape(a.shape[:-2] + (MLA_HEADS * LANES,))
    return jnp.concatenate([flat(plain), flat(rotated)], -1).astype(BF16)


def _mla_kv_kernel(ckv_ref, kpe_ref, w_ref, k_ref, v_ref):
    _mla_keys_values(ckv_ref[...], kpe_ref[...], w_ref, k_ref, v_ref)


def _mla_kv(ckv_all, kpe_all, w_ukv2):
    r = ckv_all.shape[0]
    tm = min(ROW_TILE, r)
    row = lambda w: pl.BlockSpec((tm, w), lambda i: (i, 0))
    return pl.pallas_call(
        _mla_kv_kernel, grid=(r // tm,),
        in_specs=[row(MLA_KV_LORA), row(LANES), _resident(w_ukv2.shape)],
        out_specs=[row(MLA_HEADS * LANES), row(MLA_W)],
        out_shape=[jax.ShapeDtypeStruct((r, MLA_HEADS * LANES), BF16), jax.ShapeDtypeStruct((r, MLA_W), BF16)],
        compiler_params=_params("parallel"), name="mla_kv")(ckv_all, kpe_all, w_ukv2)


def _pad_w_ukv(w_ukv):
    w = w_ukv.reshape(w_ukv.shape[:-1] + (MLA_HEADS, MLA_NOPE + MLA_V))
    nope, v = w[..., :MLA_NOPE], w[..., MLA_NOPE:]
    kpad = jnp.concatenate([nope, jnp.zeros(nope.shape[:-1] + (LANES - MLA_NOPE,), w.dtype)], -1)
    flat = lambda a: a.reshape(a.shape[:-2] + (-1,))
    return jnp.concatenate([flat(kpad), flat(v)], -1).astype(BF16)


def _flash_kernel(*refs, tq, tk, q_start, kv_len, kv_pad, causal, scale, packed_qk, has_f, heads):
    qi_ref, kj_ref, fin_ref, refs = refs[0], refs[1], refs[2], refs[3:]
    if has_f:
        q_ref, k_ref, v_ref, fq_ref, fk_ref, o_ref, m_sc, acc_sc, fq_sc = refs
    else:
        q_ref, k_ref, v_ref, o_ref, m_sc, acc_sc = refs
    step = pl.program_id(1)
    i = qi_ref[step]
    j = kj_ref[step]

    @pl.when(j == 0)
    def _():
        m_sc[...] = jnp.full(m_sc.shape, NEG_BIG, F32)
        acc_sc[...] = jnp.zeros(acc_sc.shape, F32)
        if has_f:
            for h in range(heads):
                fq_sc[h] = jnp.broadcast_to(fq_ref[0, :, FF_LANE + h:FF_LANE + h + 1], (tq, LANES))

    qmin = q_start + i * tq
    qmax = qmin + tq - 1
    kmin = j * tk
    kmax = kmin + tk - 1
    if causal == "frame":
        any_vis = kmin <= qmax
        all_vis = kmax <= qmin
    else:
        any_vis = (kmin // CHUNK) <= (qmax // CHUNK)
        all_vis = (kmax // CHUNK) <= (qmin // CHUNK)
    if kv_len < kv_pad:
        all_vis = jnp.logical_and(all_vis, kmax < kv_len)

    fold = math.log2(scale) == round(math.log2(scale))
    half_lane = lax.broadcasted_iota(jnp.int32, (1, LANES), 1) < (LANES // 2)

    def body(masked):
        if masked:
            qpos = qmin + lax.broadcasted_iota(jnp.int32, (tq, tk), 0)
            kpos = kmin + lax.broadcasted_iota(jnp.int32, (tq, tk), 1)
            if causal == "frame":
                vis = kpos <= qpos
            else:
                vis = (kpos // CHUNK) <= (qpos // CHUNK)
            if kv_len < kv_pad:
                vis = jnp.logical_and(vis, kpos < kv_len)
        nsplit = FLASH_ROW_SPLIT if tq % (FLASH_ROW_SPLIT * LANES) == 0 else 1
        rs = tq // nsplit
        units = [(h, r) for h in range(heads) for r in range(nsplit)]
        c = LOG2E if fold else LOG2E * scale
        k_cache, v_cache = {}, {}

        def k_block(b0):
            if b0 not in k_cache:
                k_cache[b0] = k_ref[0, :, b0 * LANES:(b0 + 1) * LANES].astype(BF16)
            return k_cache[b0]

        def v_block(h):
            if h not in v_cache:
                vpair = v_ref[0, :, (h // 2) * LANES:(h // 2 + 1) * LANES].astype(BF16)
                mine = half_lane if h % 2 == 0 else jnp.logical_not(half_lane)
                v_cache[h] = jnp.where(mine, vpair, jnp.ones_like(vpair))
            return v_cache[h]

        def scores(h, r):
            rows = slice(r * rs, (r + 1) * rs)
            b0 = h // 2 if packed_qk else h
            qh = q_ref[0, rows, b0 * LANES:(b0 + 1) * LANES]
            if packed_qk:
                qh = jnp.where(half_lane if h % 2 == 0 else jnp.logical_not(half_lane), qh, 0.0)
            if fold:
                qh = qh * scale
            return _dot_nt(qh.astype(BF16), k_block(b0))

        def softmax_part(h, r, s):
            rows = slice(r * rs, (r + 1) * rs)
            if has_f:
                s = s - fk_ref[0, FF_LANE + h:FF_LANE + h + 1, :]
            if masked:
                s = jnp.where(vis[rows], s, NEG_BIG)
            m_old = m_sc[h, rows]
            row_max = jnp.max(s, axis=-1, keepdims=True)
            if has_f:
                fq_b = fq_sc[h, rows]
                m_new = jnp.maximum(m_old, row_max + fq_b)
                shift = m_new - fq_b
            else:
                m_new = jnp.maximum(m_old, row_max)
                shift = m_new
            m_sc[h, rows] = m_new
            alpha = jnp.exp2((m_old - m_new) * c)
            e = jnp.concatenate([jnp.exp2((s[:, b * LANES:(b + 1) * LANES] - shift) * c)
                                 for b in range(tk // LANES)], -1)
            return alpha, e.astype(BF16)

        def weighted_values(h, r, alpha, e):
            rows = slice(r * rs, (r + 1) * rs)
            acc_sc[h, rows] = acc_sc[h, rows] * alpha + _dot(e, v_block(h))

        s_q, e_q = {}, {}
        for n in range(len(units) + 2):
            if n < len(units):
                s_q[n] = scores(*units[n])
            if 0 <= n - 1 < len(units):
                e_q[n - 1] = softmax_part(*units[n - 1], s_q.pop(n - 1))
            if 0 <= n - 2 < len(units):
                weighted_values(*units[n - 2], *e_q.pop(n - 2))

    @pl.when(all_vis)
    def _():
        body(False)

    @pl.when(jnp.logical_and(any_vis, jnp.logical_not(all_vis)))
    def _():
        body(True)

    @pl.when(fin_ref[step] == 1)
    def _():
        for pair in range(heads // 2):
            a_lo = acc_sc[2 * pair]
            a_hi = acc_sc[2 * pair + 1]
            lo = a_lo / pltpu.roll(a_lo, LANES // 2, 1)
            hi = a_hi / pltpu.roll(a_hi, LANES // 2, 1)
            o_ref[0, :, pair * LANES:(pair + 1) * LANES] = jnp.where(half_lane, lo, hi).astype(o_ref.dtype)


def _flash(q, k, v, *, causal, scale, packed_qk, kv_len, fq=None, fk=None, tq=None, tk=None, kv_row0=0):
    b, t, wq = q.shape
    lp = k.shape[1]
    heads = v.shape[2] // (LANES // 2)
    tq = tq or min(ATTN_BLOCK, t)
    tk = tk or (min(ATTN_BLOCK, lp) if t >= ATTN_BLOCK else lp)
    q_start = kv_len - t
    nq, nk = t // tq, lp // tk

    def last_block(i):
        qmax = q_start + i * tq + tq - 1
        last = qmax if causal == "frame" else (qmax // CHUNK) * CHUNK + CHUNK - 1
        return min(last // tk, nk - 1)

    pairs = [(i, j) for i in range(nq) for j in range(last_block(i) + 1)]
    qi = jnp.array([i for i, _ in pairs], jnp.int32)
    kj = jnp.array([j for _, j in pairs], jnp.int32)
    fin = jnp.array([int(j == last_block(i)) for i, j in pairs], jnp.int32)

    kv_map = lambda bi, p, qi, kj, fin: (kv_row0 + bi, kj[p], 0)
    q_map = lambda bi, p, qi, kj, fin: (bi, qi[p], 0)
    in_specs = [pl.BlockSpec((1, tq, wq), q_map),
                pl.BlockSpec((1, tk, wq), kv_map),
                pl.BlockSpec((1, tk, v.shape[2]), kv_map)]
    args = [q, k, v]
    has_f = fq is not None
    if has_f:
        in_specs += [pl.BlockSpec((1, tq, LANES), lambda bi, p, qi, kj, fin: (bi, qi[p] + q_start // tq, 0)),
                     pl.BlockSpec((1, SUBLANES, tk), lambda bi, p, qi, kj, fin: (bi, 0, kj[p]))]
        args += [fq, fk]
        assert q_start % tq == 0
    kern = functools.partial(_flash_kernel, tq=tq, tk=tk, q_start=q_start, kv_len=kv_len, kv_pad=lp,
                             causal=causal, scale=scale, packed_qk=packed_qk, has_f=has_f, heads=heads)
    grid_spec = pltpu.PrefetchScalarGridSpec(
        num_scalar_prefetch=3, grid=(b, len(pairs)), in_specs=in_specs,
        out_specs=pl.BlockSpec((1, tq, v.shape[2]), q_map),
        scratch_shapes=[pltpu.VMEM((heads, tq, LANES), F32)] * (3 if has_f else 2))
    return pl.pallas_call(
        kern, grid_spec=grid_spec,
        out_shape=jax.ShapeDtypeStruct((b, t, v.shape[2]), BF16),
        compiler_params=_params("parallel", "arbitrary"),
        name="flash_" + causal)(qi, kj, fin, *args)


def _fox_scan_kernel(*refs, p_len, t_len, l_pad):
    if p_len:
        small_ref, bias_ref, past_ref, logf_ref, fc_ref, fct_ref = refs
    else:
        small_ref, bias_ref, logf_ref, fc_ref, fct_ref = refs
    x = small_ref[0] + bias_ref[...]
    logf = jnp.minimum(x, 0.0) - jnp.log1p(jnp.exp(-jnp.abs(x)))
    logf_ref[0] = logf
    bs = SCAN_BLOCK
    tri = _tri(bs)
    carry = jnp.zeros((1, LANES), F32)
    for blk in range(l_pad // bs):
        lo = blk * bs
        if lo + bs <= p_len:
            xb = past_ref[0, lo:lo + bs, :]
        elif lo >= p_len and lo + bs <= p_len + t_len:
            xb = logf[lo - p_len:lo - p_len + bs]
        else:
            assert lo == p_len and t_len < bs
            xb = jnp.concatenate([logf, jnp.zeros((bs - t_len, LANES), F32)], 0)
        cs = _cumsum_rows(tri, xb) + carry
        carry = cs[bs - 1:bs, :]
        fc_ref[0, lo:lo + bs, :] = cs
        fct_ref[0, :, lo:lo + bs] = cs.T[:SUBLANES, :]


def _fox_scan(small3, bias128, past128, l_pad):
    b, t, _ = small3.shape
    p_len = 0 if past128 is None else past128.shape[1]
    per_b = lambda r, w: pl.BlockSpec((1, r, w), lambda bi: (bi, 0, 0))
    in_specs = [per_b(t, LANES), _resident((1, LANES))]
    args = [small3, bias128]
    if p_len:
        in_specs.append(per_b(p_len, LANES))
        args.append(past128)
    return pl.pallas_call(
        functools.partial(_fox_scan_kernel, p_len=p_len, t_len=t, l_pad=l_pad), grid=(b,),
        in_specs=in_specs,
        out_specs=[per_b(t, LANES), per_b(l_pad, LANES), per_b(SUBLANES, l_pad)],
        out_shape=[jax.ShapeDtypeStruct((b, t, LANES), F32), jax.ShapeDtypeStruct((b, l_pad, LANES), F32),
                   jax.ShapeDtypeStruct((b, SUBLANES, l_pad), F32)],
        compiler_params=_params("parallel"), name="fox_scan")(*args)


GDN_BLOCK = 256
GDN_STREAMS = 2


def _gdn_kernel(qkv_ref, z_ref, small_ref, convw_ref, gate_ref, ng_ref, conv0_ref, s0_ref,
                o_ref, sout_ref, prev_sc, s_sc, *, nb, blk, c_len):
    step = pl.program_id(1)
    nsteps = pl.num_programs(1)
    nsub = blk // c_len

    @pl.when(step == 0)
    def _():
        prev_sc[...] = conv0_ref[...]
        s_sc[...] = s0_ref[...]

    def causal_conv(rows_in):
        out = rows_in * convw_ref[CONV_W - 1:CONV_W, :]
        for s in range(1, CONV_W):
            out = out + pltpu.roll(rows_in, s, 0) * convw_ref[CONV_W - 1 - s:CONV_W - s, :]
        return out

    ri = lax.broadcasted_iota(jnp.int32, (blk, blk), 0)
    ci = lax.broadcasted_iota(jnp.int32, (blk, blk), 1)
    same = (ri // c_len) == (ci // c_len)
    incl = jnp.logical_and(ri >= ci, same)
    strict = jnp.logical_and(ri > ci, same)
    tri = incl.astype(BF16)
    nlev = int(math.log2(c_len))
    assert 2 ** nlev == c_len

    y_b, gcum_b, gcum_t_b, eg_b, ekd_b, beta_b = [], [], [], [], [], []
    for bb in range(nb):
        x = qkv_ref[bb]
        first = causal_conv(jnp.concatenate([prev_sc[bb], x[:SUBLANES]], 0))[SUBLANES:]
        y = jnp.concatenate([first, causal_conv(x)[SUBLANES:]], 0)
        prev_sc[bb] = x[blk - SUBLANES:]
        y_b.append(y * _sigmoid(y))
        small = small_ref[bb]
        g128 = -jnp.exp(gate_ref[0:1, :]) * _softplus(small + gate_ref[1:2, :])
        beta_b.append(_sigmoid(small))
        gcum = _cumsum_rows(tri, g128)
        glast = jnp.concatenate(
            [jnp.broadcast_to(gcum[(c + 1) * c_len - 1:(c + 1) * c_len, :], (c_len, LANES)) for c in range(nsub)], 0)
        gcum_b.append(gcum)
        gcum_t_b.append(gcum.T)
        eg_b.append(jnp.exp(gcum))
        ekd_b.append(jnp.exp(glast - gcum))

    units = [(bb, h) for bb in range(nb) for h in range(GDN_HEADS)]
    ids = range(len(units))
    kb_l, a_l, decay_l, rhs_l, qcb_l, qd_l, kdt_l = [], [], [], [], [], [], []
    for bb, h in units:
        y = y_b[bb]
        qh = y[:, h * GDN_DK:(h + 1) * GDN_DK]
        kh = y[:, GDN_QK + h * GDN_DK:GDN_QK + (h + 1) * GDN_DK]
        vh = y[:, 2 * GDN_QK + h * GDN_DV:2 * GDN_QK + (h + 1) * GDN_DV]
        qh = qh * lax.rsqrt(jnp.sum(qh * qh, axis=-1, keepdims=True) + L2_EPS)
        kh = kh * lax.rsqrt(jnp.sum(kh * kh, axis=-1, keepdims=True) + L2_EPS)
        lane = GA_LANE + h
        g_col = gcum_b[bb][:, lane:lane + 1]
        g_row = gcum_t_b[bb][lane:lane + 1, :]
        eg_col = eg_b[bb][:, lane:lane + 1]
        beta = beta_b[bb][:, GB_LANE + h:GB_LANE + h + 1]
        decay = jnp.exp(jnp.where(incl, g_col - g_row, NEG_BIG))
        kb = kh.astype(BF16)
        a_l.append(jnp.where(strict, beta * _dot_nt(kb, kb) * decay, 0.0))
        rhs_l.append(jnp.concatenate([vh * beta, kh * (beta * eg_col)], -1))
        qc = qh * (GDN_DK ** -0.5)
        qcb_l.append(qc.astype(BF16))
        qd_l.append((qc * eg_col).astype(BF16))
        kdt_l.append((kh * ekd_b[bb][:, lane:lane + 1]).T.astype(BF16))
        kb_l.append(kb)
        decay_l.append(decay)

    tm_l = [-a for a in a_l]
    pb_l = [a.astype(BF16) for a in a_l]
    for _ in range(1, nlev):
        pw_l = [_dot(pb, pb) for pb in pb_l]
        pb_l = [pw.astype(BF16) for pw in pw_l]
        tm_l = [tm + pw + _dot(tm.astype(BF16), pb) for tm, pw, pb in zip(tm_l, pw_l, pb_l)]
    sol_l = [rhs + _dot(tm.astype(BF16), rhs.astype(BF16)) for tm, rhs in zip(tm_l, rhs_l)]
    u_l = [sol[:, :GDN_DV] for sol in sol_l]
    wb_l = [sol[:, GDN_DV:].astype(BF16) for sol in sol_l]
    qk_l = [(_dot_nt(qcb, kb) * decay).astype(BF16) for qcb, kb, decay in zip(qcb_l, kb_l, decay_l)]

    s_l = [s_sc[bb, h] for bb, h in units]
    v_parts = [[] for _ in ids]
    qs_parts = [[] for _ in ids]
    for c in range(nsub):
        rows = slice(c * c_len, (c + 1) * c_len)
        last = slice((c + 1) * c_len - 1, (c + 1) * c_len)
        r_l = [_dot(jnp.concatenate([wb_l[n][rows], qd_l[n][rows]], 0), s_l[n].astype(BF16)) for n in ids]
        for n in ids:
            v_parts[n].append(u_l[n][rows] - r_l[n][:c_len])
            qs_parts[n].append(r_l[n][c_len:])
        s_l = [s_l[n] * eg_b[bb][last, GA_LANE + h:GA_LANE + h + 1]
               + _dot(kdt_l[n][:, rows], v_parts[n][c].astype(BF16)) for n, (bb, h) in enumerate(units)]
    for n, (bb, h) in enumerate(units):
        s_sc[bb, h] = s_l[n]
        v_all = v_parts[n][0] if nsub == 1 else jnp.concatenate(v_parts[n], 0)
        qs_all = qs_parts[n][0] if nsub == 1 else jnp.concatenate(qs_parts[n], 0)
        o = qs_all + _dot(qk_l[n], v_all.astype(BF16))
        zh = z_ref[bb, :, h * GDN_DV:(h + 1) * GDN_DV]
        o = _rms(o, ng_ref[...]) * (zh * _sigmoid(zh))
        o_ref[bb, :, h * GDN_DV:(h + 1) * GDN_DV] = o.astype(o_ref.dtype)

    @pl.when(step == nsteps - 1)
    def _():
        sout_ref[...] = s_sc[...]


def _gdn(p_gdn3, small3, conv_w, gate2, norm_g, conv0, s0, c_len):
    b, t, _ = p_gdn3.shape
    blk = min(GDN_BLOCK, t)
    nb = GDN_STREAMS if b % GDN_STREAMS == 0 else 1
    state = (nb, GDN_HEADS, GDN_DK, GDN_DV)
    return pl.pallas_call(
        functools.partial(_gdn_kernel, nb=nb, blk=blk, c_len=c_len), grid=(b // nb, t // blk),
        in_specs=[pl.BlockSpec((nb, blk, GDN_CONV_DIM), lambda bi, c: (bi, c, 0)),
                  pl.BlockSpec((nb, blk, GDN_V), lambda bi, c: (bi, c, GDN_CONV_DIM // GDN_V)),
                  pl.BlockSpec((nb, blk, LANES), lambda bi, c: (bi, c, 0)),
                  _resident((CONV_W, GDN_CONV_DIM)), _resident((2, LANES)), _resident((1, GDN_DV)),
                  pl.BlockSpec((nb, SUBLANES, GDN_CONV_DIM), lambda bi, c: (bi, 0, 0)),
                  pl.BlockSpec(state, lambda bi, c: (bi, 0, 0, 0))],
        out_specs=[pl.BlockSpec((nb, blk, GDN_V), lambda bi, c: (bi, c, 0)),
                   pl.BlockSpec(state, lambda bi, c: (bi, 0, 0, 0))],
        out_shape=[jax.ShapeDtypeStruct((b, t, GDN_V), BF16),
                   jax.ShapeDtypeStruct((b, GDN_HEADS, GDN_DK, GDN_DV), F32)],
        scratch_shapes=[pltpu.VMEM((nb, SUBLANES, GDN_CONV_DIM), F32), pltpu.VMEM(state, F32)],
        compiler_params=_params("parallel", "arbitrary"), name="gdn")(
            p_gdn3, p_gdn3, small3, conv_w, gate2, norm_g.reshape(1, -1), conv0, s0)


def _out_ln_kernel(oa_ref, ob_ref, oc_ref, x_ref, w_ref, g_ref, b_ref, wq_ref, x1_ref, qm_ref, *, alpha):
    mix = (_dot(oa_ref[...], w_ref[:MLA_W, :]) + _dot(ob_ref[...], w_ref[MLA_W:MLA_W + GDN_V, :])
           + _dot(oc_ref[...], w_ref[MLA_W + GDN_V:, :]))
    x1 = _layer_norm(alpha * x_ref[...] + mix, g_ref[...], b_ref[...])
    x1_ref[...] = x1
    qm_ref[...] = _dot(x1.astype(BF16), wq_ref[...]).astype(BF16)


def _out_ln(o_a, o_b, o_c, x2d, w_out, g, b, w_xq, alpha):
    m, d = x2d.shape
    tm = min(ROW_TILE, m)
    row = lambda w: pl.BlockSpec((tm, w), lambda i: (i, 0))
    return pl.pallas_call(
        functools.partial(_out_ln_kernel, alpha=alpha), grid=(m // tm,),
        in_specs=[row(MLA_W), row(GDN_V), row(FOX_W), row(d), _resident(w_out.shape), _resident((1, d)),
                  _resident((1, d)), _resident(w_xq.shape)],
        out_specs=[row(d), row(MEM_W)],
        out_shape=[jax.ShapeDtypeStruct((m, d), F32), jax.ShapeDtypeStruct((m, MEM_W), BF16)],
        compiler_params=_params("parallel"), name="out_ln")(
            o_a, o_b, o_c, x2d, w_out, g.reshape(1, d), b.reshape(1, d), w_xq)


def _mem_kv_kernel(m_ref, w_ref, k_ref, v_ref):
    kv = _dot(m_ref[...].astype(BF16), w_ref[...])
    k_ref[...] = kv[:, :MEM_W]
    v_ref[...] = kv[:, MEM_W:]


def _mem_kv(mem2d, w_mkv):
    m, d = mem2d.shape
    depth = w_mkv.shape[0]
    tm = min(ROW_TILE, m)
    slab = pl.BlockSpec((None, tm, MEM_W), lambda l, i: (l, i, 0))
    shp = jax.ShapeDtypeStruct((depth, m, MEM_W), F32)
    return pl.pallas_call(
        _mem_kv_kernel, grid=(depth, m // tm),
        in_specs=[pl.BlockSpec((tm, d), lambda l, i: (i, 0)),
                  pl.BlockSpec((None, d, 2 * MEM_W), lambda l, i: (l, 0, 0))],
        out_specs=[slab, slab], out_shape=[shp, shp],
        compiler_params=_params("parallel", "parallel"), name="mem_kv")(mem2d, w_mkv)


def _mem_attn_kernel(q_ref, k_ref, v_ref, x_ref, w_ref, g_ref, b_ref, o_ref, *, alpha):
    c = (MEM_HD ** -0.5) * LOG2E
    outs = []
    for h in range(MEM_HEADS):
        cols = slice(h * MEM_HD, (h + 1) * MEM_HD)
        s = _dot_nt(q_ref[0, :, cols], k_ref[0, :, cols].astype(BF16))
        e = jnp.exp2((s - jnp.max(s, axis=-1, keepdims=True)) * c)
        pv = _dot(e.astype(BF16), v_ref[0, :, cols].astype(BF16))
        outs.append(pv / jnp.sum(e, axis=-1, keepdims=True))
    o = jnp.concatenate(outs, -1).astype(BF16)
    y = alpha * x_ref[0] + _dot(o, w_ref[...])
    o_ref[0] = _layer_norm(y, g_ref[...], b_ref[...])


def _mem_attn(qm3, mk3, mv3, x3, w_xo, g, b, alpha, kv_row0=0):
    bsz, t, d = x3.shape
    tq = min(ROW_TILE, t)
    nm = mk3.shape[1]
    return pl.pallas_call(
        functools.partial(_mem_attn_kernel, alpha=alpha), grid=(bsz, t // tq),
        in_specs=[pl.BlockSpec((1, tq, MEM_W), lambda bi, i: (bi, i, 0)),
                  pl.BlockSpec((1, nm, MEM_W), lambda bi, i: (kv_row0 + bi, 0, 0)),
                  pl.BlockSpec((1, nm, MEM_W), lambda bi, i: (kv_row0 + bi, 0, 0)),
                  pl.BlockSpec((1, tq, d), lambda bi, i: (bi, i, 0)),
                  _resident(w_xo.shape), _resident((1, d)), _resident((1, d))],
        out_specs=pl.BlockSpec((1, tq, d), lambda bi, i: (bi, i, 0)),
        out_shape=jax.ShapeDtypeStruct((bsz, t, d), F32),
        compiler_params=_params("parallel", "parallel"), name="mem_attn")(
            qm3, mk3, mv3, x3, w_xo, g.reshape(1, d), b.reshape(1, d))


FFN_CHUNK = 1024


def _ffn_kernel(x_ref, w1_ref, w2_ref, g_ref, b_ref, o_ref, *, alpha):
    x = x_ref[...]
    xb = x.astype(BF16)
    acc = alpha * x
    for c in range(D_FF // FFN_CHUNK):
        cols = slice(c * FFN_CHUNK, (c + 1) * FFN_CHUNK)
        hdn = jnp.maximum(_dot(xb, w1_ref[:, cols]), 0.0)
        acc = acc + _dot((hdn * hdn).astype(BF16), w2_ref[cols, :])
    o_ref[...] = _layer_norm(acc, g_ref[...], b_ref[...])


def _ffn(x2d, w1, w2, g, b, alpha):
    m, d = x2d.shape
    tm = min(ROW_TILE, m)
    row = pl.BlockSpec((tm, d), lambda i: (i, 0))
    return pl.pallas_call(
        functools.partial(_ffn_kernel, alpha=alpha), grid=(m // tm,),
        in_specs=[row, _resident(w1.shape), _resident(w2.shape), _resident((1, d)), _resident((1, d))],
        out_specs=row, out_shape=jax.ShapeDtypeStruct((m, d), F32),
        compiler_params=_params("parallel"), name="ffn")(x2d, w1, w2, g.reshape(1, d), b.reshape(1, d))


def _round_up(n, m):
    return -(-n // m) * m


def _encoder_layer(x3, past, mem_k, mem_v, lw, rope_tabs, alpha, stack=None):
    b, t, d = x3.shape
    m = b * t
    x2d = x3.reshape(m, d)
    p_len = 0 if past is None else past[0].shape[1]
    l_len = p_len + t
    l_pad = _round_up(l_len, SCAN_BLOCK)
    cos, sin = rope_tabs

    fused_kv = past is None and l_pad == l_len
    stacked = stack is not None
    assert fused_kv or not stacked
    row0 = stack[0] * b if stacked else 0
    proj = _in_proj(x2d, lw["w_in"], cos, sin, lw["qa_g"], lw["kva_g"], lw["w_uq"],
                    lw["w_ukv"] if fused_kv else None, t, stack)
    q_a, c_kv, kpe128 = proj[:3]
    p_gdn, p_small, p_fq, p_fk, p_fv = proj[-5:]

    c_kv3 = None if stacked else c_kv.reshape(b, t, MLA_KV_LORA)
    kpe3 = kpe128.reshape(b, t, LANES)
    if fused_kv:
        k_a, v_a = proj[3:5]
    else:
        ckv_all, kpe_all = c_kv3, kpe3
        if past is not None:
            kpe_past = jnp.pad(past[1], ((0, 0), (0, 0), (ROPE_LO, LANES - ROPE_HI)))
            ckv_all = jnp.concatenate([past[0], c_kv3], 1)
            kpe_all = jnp.concatenate([kpe_past, kpe3], 1)
        if l_pad > l_len:
            ckv_all = jnp.pad(ckv_all, ((0, 0), (0, l_pad - l_len), (0, 0)))
            kpe_all = jnp.pad(kpe_all, ((0, 0), (0, l_pad - l_len), (0, 0)))
        k_a, v_a = _mla_kv(ckv_all.reshape(b * l_pad, MLA_KV_LORA), kpe_all.reshape(b * l_pad, LANES),
                           lw["w_ukv"])
    o_a = _flash(q_a.reshape(b, t, -1), k_a.reshape(b, l_pad, -1), v_a.reshape(b, l_pad, -1),
                 causal="chunk", scale=(MLA_NOPE + MLA_ROPE) ** -0.5, packed_qk=False, kv_len=l_len)

    small3 = p_small.reshape(b, t, LANES)
    p_gdn3 = p_gdn.reshape(b, t, IN_GDN_W)
    if past is None:
        conv_past = jnp.zeros((b, CONV_W - 1, GDN_CONV_DIM), F32)
        s_past = jnp.zeros((b, GDN_HEADS, GDN_DK, GDN_DV), F32)
    else:
        s_past, conv_past = past[5], past[6]
    conv0 = jnp.pad(conv_past, ((0, 0), (SUBLANES - (CONV_W - 1), 0), (0, 0)))
    o_b, s_new = _gdn(p_gdn3, small3, lw["conv_w"], lw["gate"], lw["gdn_g"], conv0, s_past, min(CHUNK, t))
    conv_new = jnp.concatenate([conv_past, p_gdn3[:, -(CONV_W - 1):, :GDN_CONV_DIM]], 1)[:, -(CONV_W - 1):]

    if past is None:
        logf_past, fk_all, fv_all = None, p_fk.reshape(-1, t, FOX_W), p_fv.reshape(-1, t, FOX_W)
    else:
        logf_past = jnp.pad(past[4], ((0, 0), (0, 0), (FF_LANE, LANES - FF_LANE - FOX_HEADS)))
        fk_all = jnp.concatenate([past[2].reshape(b, p_len, FOX_W), p_fk.reshape(b, t, FOX_W)], 1)
        fv_all = jnp.concatenate([past[3].reshape(b, p_len, FOX_W), p_fv.reshape(b, t, FOX_W)], 1)
    if l_pad > l_len:
        fk_all = jnp.pad(fk_all, ((0, 0), (0, l_pad - l_len), (0, 0)))
        fv_all = jnp.pad(fv_all, ((0, 0), (0, l_pad - l_len), (0, 0)))
    logf128, fcum, fcum_t = _fox_scan(small3, lw["fox_bias"], logf_past, l_pad)
    o_c = _flash(p_fq.reshape(b, t, FOX_W), fk_all, fv_all, causal="frame", scale=FOX_HD ** -0.5,
                 packed_qk=True, kv_len=l_len, fq=fcum, fk=fcum_t, kv_row0=row0)

    x1, qm = _out_ln(o_a.reshape(m, MLA_W), o_b.reshape(m, GDN_V), o_c.reshape(m, FOX_W), x2d, lw["w_out"],
                     lw["ln_g"][0], lw["ln_b"][0], lw["w_xq"], alpha)
    x2 = _mem_attn(qm.reshape(b, t, MEM_W), mem_k, mem_v, x1.reshape(b, t, d), lw["w_xo"],
                   lw["ln_g"][1], lw["ln_b"][1], alpha, kv_row0=row0)
    x3_new = _ffn(x2.reshape(m, d), lw["w_ff1"], lw["w_ff2"], lw["ln_g"][2], lw["ln_b"][2], alpha)

    heads_of = lambda a: None if stacked else a.reshape(b, t, FOX_HEADS, FOX_HD)
    entries = (c_kv3, kpe3[..., ROPE_LO:ROPE_HI], heads_of(p_fk), heads_of(p_fv),
               logf128[..., FF_LANE:FF_LANE + FOX_HEADS], s_new, conv_new)
    return x3_new.reshape(b, t, d), entries, ((c_kv, p_fk, p_fv) if stacked else None)


def _lane_block(vals, lane):
    return jnp.pad(vals.astype(F32), ((0, 0), (lane, LANES - lane - vals.shape[-1])))


def kernel(x_prompt, x_sample, cache_mla_ckv, cache_mla_kpe, cache_fox_k, cache_fox_v, cache_fox_logf, state_gdn, state_gdn_conv, cache_mem_k, cache_mem_v, mem_prompt, ln_in_g, ln_in_b, w_in, qa_g, kva_g, w_uq, w_ukv, gdn_conv_w, gdn_a_log, gdn_dt_bias, gdn_norm_g, fox_bf, w_out, w_xq, w_mk, w_mv, w_xo, w_ff1, w_ff2, ln_g, ln_b):
    depth = w_in.shape[0]
    alpha = (2 * depth) ** 0.25
    bp, tp, d = x_prompt.shape
    bs, ts, _ = x_sample.shape
    p_len = cache_mla_ckv.shape[2]

    w_in_p = _pad_w_in(w_in)
    w_uq_p = _pad_w_uq(w_uq)
    w_ukv_p = _pad_w_ukv(w_ukv)
    w_mkv = jnp.concatenate([w_mk, w_mv], -1).astype(BF16)
    gate = jnp.stack([_lane_block(gdn_a_log, GA_LANE), _lane_block(gdn_dt_bias, GA_LANE)], 1)
    fox_bias = _lane_block(fox_bf, FF_LANE)[:, None, :]
    bf = lambda a: a.astype(BF16)
    w_out_b, w_xq_b, w_xo_b, w_ff1_b, w_ff2_b = bf(w_out), bf(w_xq), bf(w_xo), bf(w_ff1), bf(w_ff2)

    tabs_p = _rope_tables(tp, 0)
    tabs_s = _rope_tables(ts, p_len)

    xp = _ln_in(x_prompt.reshape(bp * tp, d), ln_in_g, ln_in_b).reshape(bp, tp, d)
    xs = _ln_in(x_sample.reshape(bs * ts, d), ln_in_g, ln_in_b).reshape(bs, ts, d)
    mem2d = mem_prompt.reshape(bp * N_MEM, d)

    mk_all, mv_all = _mem_kv(mem2d, w_mkv)
    mk3 = mk_all.reshape(depth * bp, N_MEM, MEM_W)
    mv3 = mv_all.reshape(depth * bp, N_MEM, MEM_W)

    p_new, s_new = [], []
    bufs = None
    for l in range(depth):
        lw = dict(w_in=w_in_p[l], qa_g=qa_g[l], kva_g=kva_g[l], w_uq=w_uq_p[l], w_ukv=w_ukv_p[l],
                  conv_w=gdn_conv_w[l], gate=gate[l], gdn_g=gdn_norm_g[l], fox_bias=fox_bias[l],
                  w_out=w_out_b[l], w_xq=w_xq_b[l], w_xo=w_xo_b[l], w_ff1=w_ff1_b[l], w_ff2=w_ff2_b[l],
                  ln_g=ln_g[l], ln_b=ln_b[l])
        xp, ent_p, bufs = _encoder_layer(xp, None, mk3, mv3, lw, tabs_p, alpha, stack=(l, depth, bufs))
        p_new.append(ent_p)
        past = (cache_mla_ckv[l], cache_mla_kpe[l], cache_fox_k[l], cache_fox_v[l], cache_fox_logf[l],
                state_gdn[l], state_gdn_conv[l])
        xs, ent_s, _ = _encoder_layer(xs, past, cache_mem_k[l].reshape(bs, N_MEM, MEM_W),
                                      cache_mem_v[l].reshape(bs, N_MEM, MEM_W), lw, tabs_s, alpha)
        s_new.append(ent_s)

    ckv_all, fk_all, fv_all = bufs
    stack = lambda entries, i: jnp.stack([e[i] for e in entries])
    mem_shape = (depth, bp, N_MEM, MEM_HEADS, MEM_HD)
    fox_shape = (depth, bp, tp, FOX_HEADS, FOX_HD)
    return (xp, xs,
            ckv_all.reshape(depth, bp, tp, MLA_KV_LORA), stack(p_new, 1), fk_all.reshape(fox_shape),
            fv_all.reshape(fox_shape), stack(p_new, 4), stack(p_new, 5), stack(p_new, 6),
            mk_all.reshape(mem_shape), mv_all.reshape(mem_shape),
            stack(s_new, 0), stack(s_new, 1), stack(s_new, 2), stack(s_new, 3), stack(s_new, 4),
            stack(s_new, 5), stack(s_new, 6))
```

```python
import functools
import math

import jax
import jax.numpy as jnp
from jax import lax
from jax.experimental import pallas as pl
from jax.experimental.pallas import tpu as pltpu

F32 = jnp.float32
BF16 = jnp.bfloat16

D_MODEL = 1024
CHUNK = 64
MLA_HEADS = 4
MLA_NOPE = 64
MLA_ROPE = 32
MLA_V = 64
MLA_Q_LORA = 384
MLA_KV_LORA = 256
ROPE_THETA = 10000.0
GDN_HEADS = 4
GDN_DK = 128
GDN_DV = 128
CONV_W = 4
FOX_HEADS = 4
FOX_HD = 64
N_MEM = 256
MEM_HEADS = 4
MEM_HD = 128
D_FF = 4 * D_MODEL
LN_EPS = 1e-5
RMS_EPS = 1e-6
L2_EPS = 1e-6

GDN_QK = GDN_HEADS * GDN_DK
GDN_V = GDN_HEADS * GDN_DV
GDN_CONV_DIM = 2 * GDN_QK + GDN_V
FOX_W = FOX_HEADS * FOX_HD
MLA_W = MLA_HEADS * MLA_V
MEM_W = MEM_HEADS * MEM_HD

LANES = 128
SUBLANES = 8
ROW_TILE = 512
ATTN_BLOCK = 512
FLASH_ROW_SPLIT = 2
SCAN_BLOCK = 256
VMEM_LIMIT = 56 * 1024 * 1024
NEG_BIG = -1e30
LOG2E = 1.0 / math.log(2.0)

FF_LANE = 0
GA_LANE = 4
GB_LANE = 8
ROPE_LO = MLA_NOPE
ROPE_HI = MLA_NOPE + MLA_ROPE


def _params(*sem):
    return pltpu.CompilerParams(dimension_semantics=sem, vmem_limit_bytes=VMEM_LIMIT)


def _resident(shape):
    nd = len(shape)
    return pl.BlockSpec(shape, lambda *_: (0,) * nd)


def _layer_norm(y, g, b):
    mu = jnp.mean(y, axis=-1, keepdims=True)
    yc = y - mu
    var = jnp.mean(yc * yc, axis=-1, keepdims=True)
    return yc * lax.rsqrt(var + LN_EPS) * g + b


def _sigmoid(x):
    return 1.0 / (1.0 + jnp.exp(-x))


def _softplus(x):
    return jnp.maximum(x, 0.0) + jnp.log1p(jnp.exp(-jnp.abs(x)))


def _dot(a, b):
    return jnp.dot(a, b, preferred_element_type=F32)


def _dot_nt(a, b):
    return lax.dot_general(a, b, (((1,), (1,)), ((), ())), preferred_element_type=F32)


def _split3(x):
    hi = x.astype(BF16)
    r1 = x - hi.astype(F32)
    mid = r1.astype(BF16)
    lo = (r1 - mid.astype(F32)).astype(BF16)
    return hi, mid, lo


def _cumsum_rows(tri, x):
    hi, mid, lo = _split3(x)
    return _dot(tri, hi) + _dot(tri, mid) + _dot(tri, lo)


def _tri(n):
    r = lax.broadcasted_iota(jnp.int32, (n, n), 0)
    c = lax.broadcasted_iota(jnp.int32, (n, n), 1)
    return (r >= c).astype(BF16)


def _rope_table_kernel(cos_ref, sin_ref, *, start):
    shape = cos_ref.shape
    lane = lax.broadcasted_iota(jnp.int32, shape, 1)
    row = lax.broadcasted_iota(jnp.int32, shape, 0)
    half = MLA_ROPE // 2
    idx = ((lane - ROPE_LO) & (half - 1)).astype(F32)
    inv = jnp.exp(idx * (-(2.0 / MLA_ROPE) * math.log(ROPE_THETA)))
    ang = (row + start).astype(F32) * inv
    in_rope = (lane >= ROPE_LO) & (lane < ROPE_HI)
    cos_ref[...] = jnp.where(in_rope, jnp.cos(ang), 1.0)
    sin_ref[...] = jnp.where(in_rope, jnp.sin(ang), 0.0)


def _rope_tables(t, start):
    shp = jax.ShapeDtypeStruct((t, LANES), F32)
    return pl.pallas_call(functools.partial(_rope_table_kernel, start=start),
                          out_shape=(shp, shp), name="rope_tables")()


def _ln_kernel(x_ref, g_ref, b_ref, o_ref):
    o_ref[...] = _layer_norm(x_ref[...], g_ref[...], b_ref[...])


def _ln_in(x2d, g, b):
    m, d = x2d.shape
    tm = min(ROW_TILE, m)
    return pl.pallas_call(
        _ln_kernel, grid=(m // tm,),
        in_specs=[pl.BlockSpec((tm, d), lambda i: (i, 0)), _resident((1, d)), _resident((1, d))],
        out_specs=pl.BlockSpec((tm, d), lambda i: (i, 0)),
        out_shape=jax.ShapeDtypeStruct((m, d), F32),
        compiler_params=_params("parallel"), name="ln_in")(x2d, g.reshape(1, d), b.reshape(1, d))


IN_MLA_W = MLA_Q_LORA + MLA_KV_LORA + 2 * LANES
IN_GDN_W = GDN_CONV_DIM + GDN_V
IN_PAD_COLS = (IN_MLA_W, IN_GDN_W, LANES, FOX_W, FOX_W, FOX_W)
IN_PAD_OFFS = tuple(sum(IN_PAD_COLS[:n]) for n in range(len(IN_PAD_COLS) + 1))


def _rms(x, g):
    return x * lax.rsqrt(jnp.mean(x * x, axis=-1, keepdims=True) + RMS_EPS) * g


def _mla_keys_values(ckv, kpe, w_ref, k_ref, v_ref):
    kv = _dot(ckv.astype(BF16), w_ref[...])
    kw = MLA_HEADS * LANES
    for h in range(MLA_HEADS):
        k_ref[:, h * LANES:(h + 1) * LANES] = (kv[:, h * LANES:(h + 1) * LANES] + kpe).astype(BF16)
    v_ref[...] = kv[:, kw:].astype(BF16)


def _in_proj_kernel(x_ref, w_ref, cos_ref, sin_ref, qg_ref, kg_ref, wq_ref, *rest, with_kv, n_alias, fill):
    def put(ref, val):
        if fill is None:
            ref[...] = val
        else:
            for l in range(fill[1]):
                ref[l] = val if l == fill[0] else jnp.zeros_like(val)

    if with_kv:
        wkv_ref, rest = rest[0], rest[1:]
    rest = rest[n_alias:]
    if with_kv:
        q_ref, ckv_ref, kpe_ref, k_ref, v_ref = rest[:5]
    else:
        q_ref, ckv_ref, kpe_ref = rest[:3]
    o_refs = rest[-(len(IN_PAD_COLS) - 1):]
    xb = x_ref[...].astype(BF16)
    p = _dot(xb, w_ref[:, :IN_MLA_W])
    cos = cos_ref[...]
    sin = sin_ref[...]
    c_q = _rms(p[:, :MLA_Q_LORA], qg_ref[...]).astype(BF16)
    qq = _dot(c_q, wq_ref[...])
    hw = MLA_HEADS * LANES
    for h in range(MLA_HEADS):
        a = qq[:, h * LANES:(h + 1) * LANES]
        b = qq[:, hw + h * LANES:hw + (h + 1) * LANES]
        q_ref[:, h * LANES:(h + 1) * LANES] = (a * cos + b * sin).astype(BF16)
    lo = MLA_Q_LORA
    ckv = _rms(p[:, lo:lo + MLA_KV_LORA], kg_ref[...])
    lo += MLA_KV_LORA
    kpe = p[:, lo:lo + LANES] * cos + p[:, lo + LANES:lo + 2 * LANES] * sin
    put(ckv_ref, ckv)
    kpe_ref[...] = kpe
    if with_kv:
        _mla_keys_values(ckv, kpe, wkv_ref, k_ref, v_ref)
    n_slab = 2
    for n, (o_ref, lo, hi) in enumerate(zip(o_refs, IN_PAD_OFFS[1:-1], IN_PAD_OFFS[2:])):
        val = _dot(xb, w_ref[:, lo:hi])
        if n >= len(o_refs) - n_slab:
            put(o_ref, val)
        else:
            o_ref[...] = val


def _in_proj(x2d, w_pad, cos, sin, qa_g, kva_g, w_uq2, w_ukv2, t, stack=None):
    m, d = x2d.shape
    tm = min(ROW_TILE, m)
    if t < tm:
        cos, sin = jnp.tile(cos, (tm // t, 1)), jnp.tile(sin, (tm // t, 1))
    nt = max(t // tm, 1)
    with_kv = w_ukv2 is not None
    row = lambda w: pl.BlockSpec((tm, w), lambda i: (i, 0))
    tab = pl.BlockSpec((tm, LANES), lambda i: (i % nt, 0))
    in_specs = [row(d), _resident(w_pad.shape), tab, tab, _resident((1, MLA_Q_LORA)), _resident((1, MLA_KV_LORA)),
                _resident(w_uq2.shape)]
    args = [x2d, w_pad, cos, sin, qa_g.reshape(1, -1), kva_g.reshape(1, -1), w_uq2]
    outs = [(MLA_HEADS * LANES, BF16, False), (MLA_KV_LORA, F32, True), (LANES, F32, False)]
    if with_kv:
        in_specs.append(_resident(w_ukv2.shape))
        args.append(w_ukv2)
        outs += [(MLA_HEADS * LANES, BF16, False), (MLA_W, BF16, False)]
    outs += [(w, F32, False) for w in IN_PAD_COLS[1:-2]] + [(FOX_W, F32, True), (FOX_W, F32, True)]
    if stack is None:
        outs = [(w, dt, False) for w, dt, _ in outs]
        layer = depth = bufs = None
    else:
        layer, depth, bufs = stack
    aliases = {}
    fill = None
    if stack is not None and bufs is None:
        fill = (layer, depth)
        slab = lambda w: pl.BlockSpec((depth, tm, w), lambda i: (0, i, 0))
    else:
        slab = lambda w: pl.BlockSpec((None, tm, w), lambda i: (layer, i, 0))
    if bufs is not None:
        slab_outs = [n for n, (_, _, stacked) in enumerate(outs) if stacked]
        aliases = {len(args) + k: n for k, n in enumerate(slab_outs)}
        in_specs += [pl.BlockSpec(memory_space=pl.ANY)] * len(bufs)
        args += list(bufs)
    return pl.pallas_call(
        functools.partial(_in_proj_kernel, with_kv=with_kv, n_alias=len(aliases), fill=fill), grid=(m // tm,),
        in_specs=in_specs,
        out_specs=[slab(w) if stacked else row(w) for w, _, stacked in outs],
        out_shape=[jax.ShapeDtypeStruct((depth, m, w) if stacked else (m, w), dt) for w, dt, stacked in outs],
        input_output_aliases=aliases,
        compiler_params=_params("parallel"), name="in_proj")(*args)


def _pad_w_in(w_in):
    offs = [0]
    for s in (MLA_Q_LORA, MLA_KV_LORA, MLA_ROPE, GDN_QK, GDN_QK, GDN_V, GDN_V, GDN_HEADS, GDN_HEADS,
              FOX_W, FOX_W, FOX_W, FOX_HEADS):
        offs.append(offs[-1] + s)
    cq, ckv, kpe, gq, gk, gv, gz, ga, gb, fq, fk, fv, ff = [w_in[..., a:b] for a, b in zip(offs[:-1], offs[1:])]
    half = MLA_ROPE // 2
    kpe_rot = jnp.concatenate([-kpe[..., half:], kpe[..., :half]], -1)

    def z(n):
        return jnp.zeros(w_in.shape[:-1] + (n,), w_in.dtype)

    def rope_block(w):
        return jnp.concatenate([z(ROPE_LO), w, z(LANES - ROPE_HI)], -1)

    small = jnp.concatenate([ff, ga, gb, z(LANES - 3 * GDN_HEADS)], -1)
    return jnp.concatenate([cq, ckv, rope_block(kpe), rope_block(kpe_rot), gq, gk, gv, gz, small, fq, fk, fv],
                           -1).astype(BF16)


def _pad_w_uq(w_uq):
    half = MLA_ROPE // 2
    w = w_uq.reshape(w_uq.shape[:-1] + (MLA_HEADS, MLA_NOPE + MLA_ROPE))
    nope, rp = w[..., :MLA_NOPE], w[..., MLA_NOPE:]
    rot = jnp.concatenate([-rp[..., half:], rp[..., :half]], -1)
    zpad = jnp.zeros(w.shape[:-1] + (LANES - ROPE_HI,), w.dtype)
    plain = jnp.concatenate([nope, rp, zpad], -1)
    rotated = jnp.concatenate([jnp.zeros_like(nope), rot, zpad], -1)
    flat = lambda a: a.reshape(a.shape[:-2] + (MLA_HEADS * LANES,))
    return jnp.concatenate([flat(plain), flat(rotated)], -1).astype(BF16)


def _mla_kv_kernel(ckv_ref, kpe_ref, w_ref, k_ref, v_ref):
    _mla_keys_values(ckv_ref[...], kpe_ref[...], w_ref, k_ref, v_ref)


def _mla_kv(ckv_all, kpe_all, w_ukv2):
    r = ckv_all.shape[0]
    tm = min(ROW_TILE, r)
    row = lambda w: pl.BlockSpec((tm, w), lambda i: (i, 0))
    return pl.pallas_call(
        _mla_kv_kernel, grid=(r // tm,),
        in_specs=[row(MLA_KV_LORA), row(LANES), _resident(w_ukv2.shape)],
        out_specs=[row(MLA_HEADS * LANES), row(MLA_W)],
        out_shape=[jax.ShapeDtypeStruct((r, MLA_HEADS * LANES), BF16), jax.ShapeDtypeStruct((r, MLA_W), BF16)],
        compiler_params=_params("parallel"), name="mla_kv")(ckv_all, kpe_all, w_ukv2)


def _pad_w_ukv(w_ukv):
    w = w_ukv.reshape(w_ukv.shape[:-1] + (MLA_HEADS, MLA_NOPE + MLA_V))
    nope, v = w[..., :MLA_NOPE], w[..., MLA_NOPE:]
    kpad = jnp.concatenate([nope, jnp.zeros(nope.shape[:-1] + (LANES - MLA_NOPE,), w.dtype)], -1)
    flat = lambda a: a.reshape(a.shape[:-2] + (-1,))
    return jnp.concatenate([flat(kpad), flat(v)], -1).astype(BF16)


def _flash_kernel(*refs, tq, tk, q_start, kv_len, kv_pad, causal, scale, packed_qk, has_f, heads):
    qi_ref, kj_ref, fin_ref, refs = refs[0], refs[1], refs[2], refs[3:]
    if has_f:
        q_ref, k_ref, v_ref, fq_ref, fk_ref, o_ref, m_sc, acc_sc, fq_sc = refs
    else:
        q_ref, k_ref, v_ref, o_ref, m_sc, acc_sc = refs
    step = pl.program_id(1)
    i = qi_ref[step]
    j = kj_ref[step]

    @pl.when(j == 0)
    def _():
        m_sc[...] = jnp.full(m_sc.shape, NEG_BIG, F32)
        acc_sc[...] = jnp.zeros(acc_sc.shape, F32)
        if has_f:
            for h in range(heads):
                fq_sc[h] = jnp.broadcast_to(fq_ref[0, :, FF_LANE + h:FF_LANE + h + 1], (tq, LANES))

    qmin = q_start + i * tq
    qmax = qmin + tq - 1
    kmin = j * tk
    kmax = kmin + tk - 1
    if causal == "frame":
        any_vis = kmin <= qmax
        all_vis = kmax <= qmin
    else:
        any_vis = (kmin // CHUNK) <= (qmax // CHUNK)
        all_vis = (kmax // CHUNK) <= (qmin // CHUNK)
    if kv_len < kv_pad:
        all_vis = jnp.logical_and(all_vis, kmax < kv_len)

    fold = math.log2(scale) == round(math.log2(scale))
    half_lane = lax.broadcasted_iota(jnp.int32, (1, LANES), 1) < (LANES // 2)

    def body(masked):
        if masked:
            qpos = qmin + lax.broadcasted_iota(jnp.int32, (tq, tk), 0)
            kpos = kmin + lax.broadcasted_iota(jnp.int32, (tq, tk), 1)
            if causal == "frame":
                vis = kpos <= qpos
            else:
                vis = (kpos // CHUNK) <= (qpos // CHUNK)
            if kv_len < kv_pad:
                vis = jnp.logical_and(vis, kpos < kv_len)
        nsplit = FLASH_ROW_SPLIT if tq % (FLASH_ROW_SPLIT * LANES) == 0 else 1
        rs = tq // nsplit
        units = [(h, r) for h in range(heads) for r in range(nsplit)]
        c = LOG2E if fold else LOG2E * scale
        k_cache, v_cache = {}, {}

        def k_block(b0):
            if b0 not in k_cache:
                k_cache[b0] = k_ref[0, :, b0 * LANES:(b0 + 1) * LANES].astype(BF16)
            return k_cache[b0]

        def v_block(h):
            if h not in v_cache:
                vpair = v_ref[0, :, (h // 2) * LANES:(h // 2 + 1) * LANES].astype(BF16)
                mine = half_lane if h % 2 == 0 else jnp.logical_not(half_lane)
                v_cache[h] = jnp.where(mine, vpair, jnp.ones_like(vpair))
            return v_cache[h]

        def scores(h, r):
            rows = slice(r * rs, (r + 1) * rs)
            b0 = h // 2 if packed_qk else h
            qh = q_ref[0, rows, b0 * LANES:(b0 + 1) * LANES]
            if packed_qk:
                qh = jnp.where(half_lane if h % 2 == 0 else jnp.logical_not(half_lane), qh, 0.0)
            if fold:
                qh = qh * scale
            return _dot_nt(qh.astype(BF16), k_block(b0))

        def softmax_part(h, r, s):
            rows = slice(r * rs, (r + 1) * rs)
            if has_f:
                s = s - fk_ref[0, FF_LANE + h:FF_LANE + h + 1, :]
            if masked:
                s = jnp.where(vis[rows], s, NEG_BIG)
            m_old = m_sc[h, rows]
            row_max = jnp.max(s, axis=-1, keepdims=True)
            if has_f:
                fq_b = fq_sc[h, rows]
                m_new = jnp.maximum(m_old, row_max + fq_b)
                shift = m_new - fq_b
            else:
                m_new = jnp.maximum(m_old, row_max)
                shift = m_new
            m_sc[h, rows] = m_new
            alpha = jnp.exp2((m_old - m_new) * c)
            e = jnp.concatenate([jnp.exp2((s[:, b * LANES:(b + 1) * LANES] - shift) * c)
                                 for b in range(tk // LANES)], -1)
            return alpha, e.astype(BF16)

        def weighted_values(h, r, alpha, e):
            rows = slice(r * rs, (r + 1) * rs)
            acc_sc[h, rows] = acc_sc[h, rows] * alpha + _dot(e, v_block(h))

        s_q, e_q = {}, {}
        for n in range(len(units) + 2):
            if n < len(units):
                s_q[n] = scores(*units[n])
            if 0 <= n - 1 < len(units):
                e_q[n - 1] = softmax_part(*units[n - 1], s_q.pop(n - 1))
            if 0 <= n - 2 < len(units):
                weighted_values(*units[n - 2], *e_q.pop(n - 2))

    @pl.when(all_vis)
    def _():
        body(False)

    @pl.when(jnp.logical_and(any_vis, jnp.logical_not(all_vis)))
    def _():
        body(True)

    @pl.when(fin_ref[step] == 1)
    def _():
        for pair in range(heads // 2):
            a_lo = acc_sc[2 * pair]
            a_hi = acc_sc[2 * pair + 1]
            lo = a_lo / pltpu.roll(a_lo, LANES // 2, 1)
            hi = a_hi / pltpu.roll(a_hi, LANES // 2, 1)
            o_ref[0, :, pair * LANES:(pair + 1) * LANES] = jnp.where(half_lane, lo, hi).astype(o_ref.dtype)


def _flash(q, k, v, *, causal, scale, packed_qk, kv_len, fq=None, fk=None, tq=None, tk=None, kv_row0=0):
    b, t, wq = q.shape
    lp = k.shape[1]
    heads = v.shape[2] // (LANES // 2)
    tq = tq or min(ATTN_BLOCK, t)
    tk = tk or (min(ATTN_BLOCK, lp) if t >= ATTN_BLOCK else lp)
    q_start = kv_len - t
    nq, nk = t // tq, lp // tk

    def last_block(i):
        qmax = q_start + i * tq + tq - 1
        last = qmax if causal == "frame" else (qmax // CHUNK) * CHUNK + CHUNK - 1
        return min(last // tk, nk - 1)

    pairs = [(i, j) for i in range(nq) for j in range(last_block(i) + 1)]
    qi = jnp.array([i for i, _ in pairs], jnp.int32)
    kj = jnp.array([j for _, j in pairs], jnp.int32)
    fin = jnp.array([int(j == last_block(i)) for i, j in pairs], jnp.int32)

    kv_map = lambda bi, p, qi, kj, fin: (kv_row0 + bi, kj[p], 0)
    q_map = lambda bi, p, qi, kj, fin: (bi, qi[p], 0)
    in_specs = [pl.BlockSpec((1, tq, wq), q_map),
                pl.BlockSpec((1, tk, wq), kv_map),
                pl.BlockSpec((1, tk, v.shape[2]), kv_map)]
    args = [q, k, v]
    has_f = fq is not None
    if has_f:
        in_specs += [pl.BlockSpec((1, tq, LANES), lambda bi, p, qi, kj, fin: (bi, qi[p] + q_start // tq, 0)),
                     pl.BlockSpec((1, SUBLANES, tk), lambda bi, p, qi, kj, fin: (bi, 0, kj[p]))]
        args += [fq, fk]
        assert q_start % tq == 0
    kern = functools.partial(_flash_kernel, tq=tq, tk=tk, q_start=q_start, kv_len=kv_len, kv_pad=lp,
                             causal=causal, scale=scale, packed_qk=packed_qk, has_f=has_f, heads=heads)
    grid_spec = pltpu.PrefetchScalarGridSpec(
        num_scalar_prefetch=3, grid=(b, len(pairs)), in_specs=in_specs,
        out_specs=pl.BlockSpec((1, tq, v.shape[2]), q_map),
        scratch_shapes=[pltpu.VMEM((heads, tq, LANES), F32)] * (3 if has_f else 2))
    return pl.pallas_call(
        kern, grid_spec=grid_spec,
        out_shape=jax.ShapeDtypeStruct((b, t, v.shape[2]), BF16),
        compiler_params=_params("parallel", "arbitrary"),
        name="flash_" + causal)(qi, kj, fin, *args)


def _fox_scan_kernel(*refs, p_len, t_len, l_pad):
    if p_len:
        small_ref, bias_ref, past_ref, logf_ref, fc_ref, fct_ref = refs
    else:
        small_ref, bias_ref, logf_ref, fc_ref, fct_ref = refs
    x = small_ref[0] + bias_ref[...]
    logf = jnp.minimum(x, 0.0) - jnp.log1p(jnp.exp(-jnp.abs(x)))
    logf_ref[0] = logf
    bs = SCAN_BLOCK
    tri = _tri(bs)
    carry = jnp.zeros((1, LANES), F32)
    for blk in range(l_pad // bs):
        lo = blk * bs
        if lo + bs <= p_len:
            xb = past_ref[0, lo:lo + bs, :]
        elif lo >= p_len and lo + bs <= p_len + t_len:
            xb = logf[lo - p_len:lo - p_len + bs]
        else:
            assert lo == p_len and t_len < bs
            xb = jnp.concatenate([logf, jnp.zeros((bs - t_len, LANES), F32)], 0)
        cs = _cumsum_rows(tri, xb) + carry
        carry = cs[bs - 1:bs, :]
        fc_ref[0, lo:lo + bs, :] = cs
        fct_ref[0, :, lo:lo + bs] = cs.T[:SUBLANES, :]


def _fox_scan(small3, bias128, past128, l_pad):
    b, t, _ = small3.shape
    p_len = 0 if past128 is None else past128.shape[1]
    per_b = lambda r, w: pl.BlockSpec((1, r, w), lambda bi: (bi, 0, 0))
    in_specs = [per_b(t, LANES), _resident((1, LANES))]
    args = [small3, bias128]
    if p_len:
        in_specs.append(per_b(p_len, LANES))
        args.append(past128)
    return pl.pallas_call(
        functools.partial(_fox_scan_kernel, p_len=p_len, t_len=t, l_pad=l_pad), grid=(b,),
        in_specs=in_specs,
        out_specs=[per_b(t, LANES), per_b(l_pad, LANES), per_b(SUBLANES, l_pad)],
        out_shape=[jax.ShapeDtypeStruct((b, t, LANES), F32), jax.ShapeDtypeStruct((b, l_pad, LANES), F32),
                   jax.ShapeDtypeStruct((b, SUBLANES, l_pad), F32)],
        compiler_params=_params("parallel"), name="fox_scan")(*args)


GDN_BLOCK = 256
GDN_STREAMS = 2


def _gdn_kernel(qkv_ref, z_ref, small_ref, convw_ref, gate_ref, ng_ref, conv0_ref, s0_ref,
                o_ref, sout_ref, prev_sc, s_sc, *, nb, blk, c_len):
    step = pl.program_id(1)
    nsteps = pl.num_programs(1)
    nsub = blk // c_len

    @pl.when(step == 0)
    def _():
        prev_sc[...] = conv0_ref[...]
        s_sc[...] = s0_ref[...]

    def causal_conv(rows_in):
        out = rows_in * convw_ref[CONV_W - 1:CONV_W, :]
        for s in range(1, CONV_W):
            out = out + pltpu.roll(rows_in, s, 0) * convw_ref[CONV_W - 1 - s:CONV_W - s, :]
        return out

    ri = lax.broadcasted_iota(jnp.int32, (blk, blk), 0)
    ci = lax.broadcasted_iota(jnp.int32, (blk, blk), 1)
    same = (ri // c_len) == (ci // c_len)
    incl = jnp.logical_and(ri >= ci, same)
    strict = jnp.logical_and(ri > ci, same)
    tri = incl.astype(BF16)
    nlev = int(math.log2(c_len))
    assert 2 ** nlev == c_len

    y_b, gcum_b, gcum_t_b, eg_b, ekd_b, beta_b = [], [], [], [], [], []
    for bb in range(nb):
        x = qkv_ref[bb]
        first = causal_conv(jnp.concatenate([prev_sc[bb], x[:SUBLANES]], 0))[SUBLANES:]
        y = jnp.concatenate([first, causal_conv(x)[SUBLANES:]], 0)
        prev_sc[bb] = x[blk - SUBLANES:]
        y_b.append(y * _sigmoid(y))
        small = small_ref[bb]
        g128 = -jnp.exp(gate_ref[0:1, :]) * _softplus(small + gate_ref[1:2, :])
        beta_b.append(_sigmoid(small))
        gcum = _cumsum_rows(tri, g128)
        glast = jnp.concatenate(
            [jnp.broadcast_to(gcum[(c + 1) * c_len - 1:(c + 1) * c_len, :], (c_len, LANES)) for c in range(nsub)], 0)
        gcum_b.append(gcum)
        gcum_t_b.append(gcum.T)
        eg_b.append(jnp.exp(gcum))
        ekd_b.append(jnp.exp(glast - gcum))

    units = [(bb, h) for bb in range(nb) for h in range(GDN_HEADS)]
    ids = range(len(units))
    kb_l, a_l, decay_l, rhs_l, qcb_l, qd_l, kdt_l = [], [], [], [], [], [], []
    for bb, h in units:
        y = y_b[bb]
        qh = y[:, h * GDN_DK:(h + 1) * GDN_DK]
        kh = y[:, GDN_QK + h * GDN_DK:GDN_QK + (h + 1) * GDN_DK]
        vh = y[:, 2 * GDN_QK + h * GDN_DV:2 * GDN_QK + (h + 1) * GDN_DV]
        qh = qh * lax.rsqrt(jnp.sum(qh * qh, axis=-1, keepdims=True) + L2_EPS)
        kh = kh * lax.rsqrt(jnp.sum(kh * kh, axis=-1, keepdims=True) + L2_EPS)
        lane = GA_LANE + h
        g_col = gcum_b[bb][:, lane:lane + 1]
        g_row = gcum_t_b[bb][lane:lane + 1, :]
        eg_col = eg_b[bb][:, lane:lane + 1]
        beta = beta_b[bb][:, GB_LANE + h:GB_LANE + h + 1]
        decay = jnp.exp(jnp.where(incl, g_col - g_row, NEG_BIG))
        kb = kh.astype(BF16)
        a_l.append(jnp.where(strict, beta * _dot_nt(kb, kb) * decay, 0.0))
        rhs_l.append(jnp.concatenate([vh * beta, kh * (beta * eg_col)], -1))
        qc = qh * (GDN_DK ** -0.5)
        qcb_l.append(qc.astype(BF16))
        qd_l.append((qc * eg_col).astype(BF16))
        kdt_l.append((kh * ekd_b[bb][:, lane:lane + 1]).T.astype(BF16))
        kb_l.append(kb)
        decay_l.append(decay)

    tm_l = [-a for a in a_l]
    pb_l = [a.astype(BF16) for a in a_l]
    for _ in range(1, nlev):
        pw_l = [_dot(pb, pb) for pb in pb_l]
        pb_l = [pw.astype(BF16) for pw in pw_l]
        tm_l = [tm + pw + _dot(tm.astype(BF16), pb) for tm, pw, pb in zip(tm_l, pw_l, pb_l)]
    sol_l = [rhs + _dot(tm.astype(BF16), rhs.astype(BF16)) for tm, rhs in zip(tm_l, rhs_l)]
    u_l = [sol[:, :GDN_DV] for sol in sol_l]
    wb_l = [sol[:, GDN_DV:].astype(BF16) for sol in sol_l]
    qk_l = [(_dot_nt(qcb, kb) * decay).astype(BF16) for qcb, kb, decay in zip(qcb_l, kb_l, decay_l)]

    s_l = [s_sc[bb, h] for bb, h in units]
    v_parts = [[] for _ in ids]
    qs_parts = [[] for _ in ids]
    for c in range(nsub):
        rows = slice(c * c_len, (c + 1) * c_len)
        last = slice((c + 1) * c_len - 1, (c + 1) * c_len)
        r_l = [_dot(jnp.concatenate([wb_l[n][rows], qd_l[n][rows]], 0), s_l[n].astype(BF16)) for n in ids]
        for n in ids:
            v_parts[n].append(u_l[n][rows] - r_l[n][:c_len])
            qs_parts[n].append(r_l[n][c_len:])
        s_l = [s_l[n] * eg_b[bb][last, GA_LANE + h:GA_LANE + h + 1]
               + _dot(kdt_l[n][:, rows], v_parts[n][c].astype(BF16)) for n, (bb, h) in enumerate(units)]
    for n, (bb, h) in enumerate(units):
        s_sc[bb, h] = s_l[n]
        v_all = v_parts[n][0] if nsub == 1 else jnp.concatenate(v_parts[n], 0)
        qs_all = qs_parts[n][0] if nsub == 1 else jnp.concatenate(qs_parts[n], 0)
        o = qs_all + _dot(qk_l[n], v_all.astype(BF16))
        zh = z_ref[bb, :, h * GDN_DV:(h + 1) * GDN_DV]
        o = _rms(o, ng_ref[...]) * (zh * _sigmoid(zh))
        o_ref[bb, :, h * GDN_DV:(h + 1) * GDN_DV] = o.astype(o_ref.dtype)

    @pl.when(step == nsteps - 1)
    def _():
        sout_ref[...] = s_sc[...]


def _gdn(p_gdn3, small3, conv_w, gate2, norm_g, conv0, s0, c_len):
    b, t, _ = p_gdn3.shape
    blk = min(GDN_BLOCK, t)
    nb = GDN_STREAMS if b % GDN_STREAMS == 0 else 1
    state = (nb, GDN_HEADS, GDN_DK, GDN_DV)
    return pl.pallas_call(
        functools.partial(_gdn_kernel, nb=nb, blk=blk, c_len=c_len), grid=(b // nb, t // blk),
        in_specs=[pl.BlockSpec((nb, blk, GDN_CONV_DIM), lambda bi, c: (bi, c, 0)),
                  pl.BlockSpec((nb, blk, GDN_V), lambda bi, c: (bi, c, GDN_CONV_DIM // GDN_V)),
                  pl.BlockSpec((nb, blk, LANES), lambda bi, c: (bi, c, 0)),
                  _resident((CONV_W, GDN_CONV_DIM)), _resident((2, LANES)), _resident((1, GDN_DV)),
                  pl.BlockSpec((nb, SUBLANES, GDN_CONV_DIM), lambda bi, c: (bi, 0, 0)),
                  pl.BlockSpec(state, lambda bi, c: (bi, 0, 0, 0))],
        out_specs=[pl.BlockSpec((nb, blk, GDN_V), lambda bi, c: (bi, c, 0)),
                   pl.BlockSpec(state, lambda bi, c: (bi, 0, 0, 0))],
        out_shape=[jax.ShapeDtypeStruct((b, t, GDN_V), BF16),
                   jax.ShapeDtypeStruct((b, GDN_HEADS, GDN_DK, GDN_DV), F32)],
        scratch_shapes=[pltpu.VMEM((nb, SUBLANES, GDN_CONV_DIM), F32), pltpu.VMEM(state, F32)],
        compiler_params=_params("parallel", "arbitrary"), name="gdn")(
            p_gdn3, p_gdn3, small3, conv_w, gate2, norm_g.reshape(1, -1), conv0, s0)


def _out_ln_kernel(oa_ref, ob_ref, oc_ref, x_ref, w_ref, g_ref, b_ref, wq_ref, x1_ref, qm_ref, *, alpha):
    mix = (_dot(oa_ref[...], w_ref[:MLA_W, :]) + _dot(ob_ref[...], w_ref[MLA_W:MLA_W + GDN_V, :])
           + _dot(oc_ref[...], w_ref[MLA_W + GDN_V:, :]))
    x1 = _layer_norm(alpha * x_ref[...] + mix, g_ref[...], b_ref[...])
    x1_ref[...] = x1
    qm_ref[...] = _dot(x1.astype(BF16), wq_ref[...]).astype(BF16)


def _out_ln(o_a, o_b, o_c, x2d, w_out, g, b, w_xq, alpha):
    m, d = x2d.shape
    tm = min(ROW_TILE, m)
    row = lambda w: pl.BlockSpec((tm, w), lambda i: (i, 0))
    return pl.pallas_call(
        functools.partial(_out_ln_kernel, alpha=alpha), grid=(m // tm,),
        in_specs=[row(MLA_W), row(GDN_V), row(FOX_W), row(d), _resident(w_out.shape), _resident((1, d)),
                  _resident((1, d)), _resident(w_xq.shape)],
        out_specs=[row(d), row(MEM_W)],
        out_shape=[jax.ShapeDtypeStruct((m, d), F32), jax.ShapeDtypeStruct((m, MEM_W), BF16)],
        compiler_params=_params("parallel"), name="out_ln")(
            o_a, o_b, o_c, x2d, w_out, g.reshape(1, d), b.reshape(1, d), w_xq)


def _mem_kv_kernel(m_ref, w_ref, k_ref, v_ref):
    kv = _dot(m_ref[...].astype(BF16), w_ref[...])
    k_ref[...] = kv[:, :MEM_W]
    v_ref[...] = kv[:, MEM_W:]


def _mem_kv(mem2d, w_mkv):
    m, d = mem2d.shape
    depth = w_mkv.shape[0]
    tm = min(ROW_TILE, m)
    slab = pl.BlockSpec((None, tm, MEM_W), lambda l, i: (l, i, 0))
    shp = jax.ShapeDtypeStruct((depth, m, MEM_W), F32)
    return pl.pallas_call(
        _mem_kv_kernel, grid=(depth, m // tm),
        in_specs=[pl.BlockSpec((tm, d), lambda l, i: (i, 0)),
                  pl.BlockSpec((None, d, 2 * MEM_W), lambda l, i: (l, 0, 0))],
        out_specs=[slab, slab], out_shape=[shp, shp],
        compiler_params=_params("parallel", "parallel"), name="mem_kv")(mem2d, w_mkv)


def _mem_attn_kernel(q_ref, k_ref, v_ref, x_ref, w_ref, g_ref, b_ref, o_ref, *, alpha):
    c = (MEM_HD ** -0.5) * LOG2E
    outs = []
    for h in range(MEM_HEADS):
        cols = slice(h * MEM_HD, (h + 1) * MEM_HD)
        s = _dot_nt(q_ref[0, :, cols], k_ref[0, :, cols].astype(BF16))
        e = jnp.exp2((s - jnp.max(s, axis=-1, keepdims=True)) * c)
        pv = _dot(e.astype(BF16), v_ref[0, :, cols].astype(BF16))
        outs.append(pv / jnp.sum(e, axis=-1, keepdims=True))
    o = jnp.concatenate(outs, -1).astype(BF16)
    y = alpha * x_ref[0] + _dot(o, w_ref[...])
    o_ref[0] = _layer_norm(y, g_ref[...], b_ref[...])


def _mem_attn(qm3, mk3, mv3, x3, w_xo, g, b, alpha, kv_row0=0):
    bsz, t, d = x3.shape
    tq = min(ROW_TILE, t)
    nm = mk3.shape[1]
    return pl.pallas_call(
        functools.partial(_mem_attn_kernel, alpha=alpha), grid=(bsz, t // tq),
        in_specs=[pl.BlockSpec((1, tq, MEM_W), lambda bi, i: (bi, i, 0)),
                  pl.BlockSpec((1, nm, MEM_W), lambda bi, i: (kv_row0 + bi, 0, 0)),
                  pl.BlockSpec((1, nm, MEM_W), lambda bi, i: (kv_row0 + bi, 0, 0)),
                  pl.BlockSpec((1, tq, d), lambda bi, i: (bi, i, 0)),
                  _resident(w_xo.shape), _resident((1, d)), _resident((1, d))],
        out_specs=pl.BlockSpec((1, tq, d), lambda bi, i: (bi, i, 0)),
        out_shape=jax.ShapeDtypeStruct((bsz, t, d), F32),
        compiler_params=_params("parallel", "parallel"), name="mem_attn")(
            qm3, mk3, mv3, x3, w_xo, g.reshape(1, d), b.reshape(1, d))


FFN_CHUNK = 1024


def _ffn_kernel(x_ref, w1_ref, w2_ref, g_ref, b_ref, o_ref, *, alpha):
    x = x_ref[...]
    xb = x.astype(BF16)
    acc = alpha * x
    for c in range(D_FF // FFN_CHUNK):
        cols = slice(c * FFN_CHUNK, (c + 1) * FFN_CHUNK)
        hdn = jnp.maximum(_dot(xb, w1_ref[:, cols]), 0.0)
        acc = acc + _dot((hdn * hdn).astype(BF16), w2_ref[cols, :])
    o_ref[...] = _layer_norm(acc, g_ref[...], b_ref[...])


def _ffn(x2d, w1, w2, g, b, alpha):
    m, d = x2d.shape
    tm = min(ROW_TILE, m)
    row = pl.BlockSpec((tm, d), lambda i: (i, 0))
    return pl.pallas_call(
        functools.partial(_ffn_kernel, alpha=alpha), grid=(m // tm,),
        in_specs=[row, _resident(w1.shape), _resident(w2.shape), _resident((1, d)), _resident((1, d))],
        out_specs=row, out_shape=jax.ShapeDtypeStruct((m, d), F32),
        compiler_params=_params("parallel"), name="ffn")(x2d, w1, w2, g.reshape(1, d), b.reshape(1, d))


def _round_up(n, m):
    return -(-n // m) * m


def _encoder_layer(x3, past, mem_k, mem_v, lw, rope_tabs, alpha, stack=None):
    b, t, d = x3.shape
    m = b * t
    x2d = x3.reshape(m, d)
    p_len = 0 if past is None else past[0].shape[1]
    l_len = p_len + t
    l_pad = _round_up(l_len, SCAN_BLOCK)
    cos, sin = rope_tabs

    fused_kv = past is None and l_pad == l_len
    stacked = stack is not None
    assert fused_kv or not stacked
    row0 = stack[0] * b if stacked else 0
    proj = _in_proj(x2d, lw["w_in"], cos, sin, lw["qa_g"], lw["kva_g"], lw["w_uq"],
                    lw["w_ukv"] if fused_kv else None, t, stack)
    q_a, c_kv, kpe128 = proj[:3]
    p_gdn, p_small, p_fq, p_fk, p_fv = proj[-5:]

    c_kv3 = None if stacked else c_kv.reshape(b, t, MLA_KV_LORA)
    kpe3 = kpe128.reshape(b, t, LANES)
    if fused_kv:
        k_a, v_a = proj[3:5]
    else:
        ckv_all, kpe_all = c_kv3, kpe3
        if past is not None:
            kpe_past = jnp.pad(past[1], ((0, 0), (0, 0), (ROPE_LO, LANES - ROPE_HI)))
            ckv_all = jnp.concatenate([past[0], c_kv3], 1)
            kpe_all = jnp.concatenate([kpe_past, kpe3], 1)
        if l_pad > l_len:
            ckv_all = jnp.pad(ckv_all, ((0, 0), (0, l_pad - l_len), (0, 0)))
            kpe_all = jnp.pad(kpe_all, ((0, 0), (0, l_pad - l_len), (0, 0)))
        k_a, v_a = _mla_kv(ckv_all.reshape(b * l_pad, MLA_KV_LORA), kpe_all.reshape(b * l_pad, LANES),
                           lw["w_ukv"])
    o_a = _flash(q_a.reshape(b, t, -1), k_a.reshape(b, l_pad, -1), v_a.reshape(b, l_pad, -1),
                 causal="chunk", scale=(MLA_NOPE + MLA_ROPE) ** -0.5, packed_qk=False, kv_len=l_len)

    small3 = p_small.reshape(b, t, LANES)
    p_gdn3 = p_gdn.reshape(b, t, IN_GDN_W)
    if past is None:
        conv_past = jnp.zeros((b, CONV_W - 1, GDN_CONV_DIM), F32)
        s_past = jnp.zeros((b, GDN_HEADS, GDN_DK, GDN_DV), F32)
    else:
        s_past, conv_past = past[5], past[6]
    conv0 = jnp.pad(conv_past, ((0, 0), (SUBLANES - (CONV_W - 1), 0), (0, 0)))
    o_b, s_new = _gdn(p_gdn3, small3, lw["conv_w"], lw["gate"], lw["gdn_g"], conv0, s_past, min(CHUNK, t))
    conv_new = jnp.concatenate([conv_past, p_gdn3[:, -(CONV_W - 1):, :GDN_CONV_DIM]], 1)[:, -(CONV_W - 1):]

    if past is None:
        logf_past, fk_all, fv_all = None, p_fk.reshape(-1, t, FOX_W), p_fv.reshape(-1, t, FOX_W)
    else:
        logf_past = jnp.pad(past[4], ((0, 0), (0, 0), (FF_LANE, LANES - FF_LANE - FOX_HEADS)))
        fk_all = jnp.concatenate([past[2].reshape(b, p_len, FOX_W), p_fk.reshape(b, t, FOX_W)], 1)
        fv_all = jnp.concatenate([past[3].reshape(b, p_len, FOX_W), p_fv.reshape(b, t, FOX_W)], 1)
    if l_pad > l_len:
        fk_all = jnp.pad(fk_all, ((0, 0), (0, l_pad - l_len), (0, 0)))
        fv_all = jnp.pad(fv_all, ((0, 0), (0, l_pad - l_len), (0, 0)))
    logf128, fcum, fcum_t = _fox_scan(small3, lw["fox_bias"], logf_past, l_pad)
    o_c = _flash(p_fq.reshape(b, t, FOX_W), fk_all, fv_all, causal="frame", scale=FOX_HD ** -0.5,
                 packed_qk=True, kv_len=l_len, fq=fcum, fk=fcum_t, kv_row0=row0)

    x1, qm = _out_ln(o_a.reshape(m, MLA_W), o_b.reshape(m, GDN_V), o_c.reshape(m, FOX_W), x2d, lw["w_out"],
                     lw["ln_g"][0], lw["ln_b"][0], lw["w_xq"], alpha)
    x2 = _mem_attn(qm.reshape(b, t, MEM_W), mem_k, mem_v, x1.reshape(b, t, d), lw["w_xo"],
                   lw["ln_g"][1], lw["ln_b"][1], alpha, kv_row0=row0)
    x3_new = _ffn(x2.reshape(m, d), lw["w_ff1"], lw["w_ff2"], lw["ln_g"][2], lw["ln_b"][2], alpha)

    heads_of = lambda a: None if stacked else a.reshape(b, t, FOX_HEADS, FOX_HD)
    entries = (c_kv3, kpe3[..., ROPE_LO:ROPE_HI], heads_of(p_fk), heads_of(p_fv),
               logf128[..., FF_LANE:FF_LANE + FOX_HEADS], s_new, conv_new)
    return x3_new.reshape(b, t, d), entries, ((c_kv, p_fk, p_fv) if stacked else None)


def _lane_block(vals, lane):
    return jnp.pad(vals.astype(F32), ((0, 0), (lane, LANES - lane - vals.shape[-1])))


def kernel(x_prompt, x_sample, cache_mla_ckv, cache_mla_kpe, cache_fox_k, cache_fox_v, cache_fox_logf, state_gdn, state_gdn_conv, cache_mem_k, cache_mem_v, mem_prompt, ln_in_g, ln_in_b, w_in, qa_g, kva_g, w_uq, w_ukv, gdn_conv_w, gdn_a_log, gdn_dt_bias, gdn_norm_g, fox_bf, w_out, w_xq, w_mk, w_mv, w_xo, w_ff1, w_ff2, ln_g, ln_b):
    depth = w_in.shape[0]
    alpha = (2 * depth) ** 0.25
    bp, tp, d = x_prompt.shape
    bs, ts, _ = x_sample.shape
    p_len = cache_mla_ckv.shape[2]

    w_in_p = _pad_w_in(w_in)
    w_uq_p = _pad_w_uq(w_uq)
    w_ukv_p = _pad_w_ukv(w_ukv)
    w_mkv = jnp.concatenate([w_mk, w_mv], -1).astype(BF16)
    gate = jnp.stack([_lane_block(gdn_a_log, GA_LANE), _lane_block(gdn_dt_bias, GA_LANE)], 1)
    fox_bias = _lane_block(fox_bf, FF_LANE)[:, None, :]
    bf = lambda a: a.astype(BF16)
    w_out_b, w_xq_b, w_xo_b, w_ff1_b, w_ff2_b = bf(w_out), bf(w_xq), bf(w_xo), bf(w_ff1), bf(w_ff2)

    tabs_p = _rope_tables(tp, 0)
    tabs_s = _rope_tables(ts, p_len)

    xp = _ln_in(x_prompt.reshape(bp * tp, d), ln_in_g, ln_in_b).reshape(bp, tp, d)
    xs = _ln_in(x_sample.reshape(bs * ts, d), ln_in_g, ln_in_b).reshape(bs, ts, d)
    mem2d = mem_prompt.reshape(bp * N_MEM, d)

    mk_all, mv_all = _mem_kv(mem2d, w_mkv)
    mk3 = mk_all.reshape(depth * bp, N_MEM, MEM_W)
    mv3 = mv_all.reshape(depth * bp, N_MEM, MEM_W)

    p_new, s_new = [], []
    bufs = None
    for l in range(depth):
        lw = dict(w_in=w_in_p[l], qa_g=qa_g[l], kva_g=kva_g[l], w_uq=w_uq_p[l], w_ukv=w_ukv_p[l],
                  conv_w=gdn_conv_w[l], gate=gate[l], gdn_g=gdn_norm_g[l], fox_bias=fox_bias[l],
                  w_out=w_out_b[l], w_xq=w_xq_b[l], w_xo=w_xo_b[l], w_ff1=w_ff1_b[l], w_ff2=w_ff2_b[l],
                  ln_g=ln_g[l], ln_b=ln_b[l])
        xp, ent_p, bufs = _encoder_layer(xp, None, mk3, mv3, lw, tabs_p, alpha, stack=(l, depth, bufs))
        p_new.append(ent_p)
        past = (cache_mla_ckv[l], cache_mla_kpe[l], cache_fox_k[l], cache_fox_v[l], cache_fox_logf[l],
                state_gdn[l], state_gdn_conv[l])
        xs, ent_s, _ = _encoder_layer(xs, past, cache_mem_k[l].reshape(bs, N_MEM, MEM_W),
                                      cache_mem_v[l].reshape(bs, N_MEM, MEM_W), lw, tabs_s, alpha)
        s_new.append(ent_s)

    ckv_all, fk_all, fv_all = bufs
    stack = lambda entries, i: jnp.stack([e[i] for e in entries])
    mem_shape = (depth, bp, N_MEM, MEM_HEADS, MEM_HD)
    fox_shape = (depth, bp, tp, FOX_HEADS, FOX_HD)
    return (xp, xs,
            ckv_all.reshape(depth, bp, tp, MLA_KV_LORA), stack(p_new, 1), fk_all.reshape(fox_shape),
            fv_all.reshape(fox_shape), stack(p_new, 4), stack(p_new, 5), stack(p_new, 6),
            mk_all.reshape(mem_shape), mv_all.reshape(mem_shape),
            stack(s_new, 0), stack(s_new, 1), stack(s_new, 2), stack(s_new, 3), stack(s_new, 4),
            stack(s_new, 5), stack(s_new, 6))
```
